```python
import jax, jax.numpy as jnp
from jax import lax
import numpy as np

D_MODEL = 1024
BATCH = 2
SEQ = 8192
DEPTH = 1

D_MIX = D_MODEL
M_HEADS = 4
M_HEAD_DIM = 128
D_MLSTM = M_HEADS * M_HEAD_DIM
D_GMLP = D_MIX - D_MLSTM
G_GROUPS = 4
G_DIM = D_GMLP // G_GROUPS
CHUNK = 128
CONV_K = 4
N_EXPERTS = 32
TOP_K = 4
D_FF = D_MODEL
SWIGLU_LIMIT = 7.0
SWIGLU_ALPHA = 1.702
MOE_BLOCK = 128
EPS = 1e-6
PROJ_SIZES = (D_MLSTM, D_MLSTM, D_MLSTM, D_MLSTM, M_HEADS, M_HEADS, D_GMLP, D_GMLP)
D_PROJ = sum(PROJ_SIZES)

kernel_name = "hybrid_mlstm_gmlp_moe"


def rms_norm(x, gain):
    xf = x.astype(jnp.float32)
    y = xf * lax.rsqrt(jnp.mean(xf * xf, axis=-1, keepdims=True) + EPS)
    return (y * gain).astype(x.dtype)


def layer_norm(x, gain):
    xf = x.astype(jnp.float32)
    mu = jnp.mean(xf, axis=-1, keepdims=True)
    xc = xf - mu
    y = xc * lax.rsqrt(jnp.mean(xc * xc, axis=-1, keepdims=True) + EPS)
    return (y * gain).astype(x.dtype)


def causal_dwconv(x, w):
    S = x.shape[1]
    xp = jnp.pad(x, ((0, 0), (CONV_K - 1, 0), (0, 0)))
    y = xp[:, 0:S] * w[0]
    for j in range(1, CONV_K):
        y = y + xp[:, j:j + S] * w[j]
    return y


def mlstm_chunkwise(q, k, v, i_pre, f_pre):
    B, S, H, DH = q.shape
    nc = S // CHUNK
    out_dtype = q.dtype

    def chunks(t):
        return t.astype(jnp.float32).reshape(B, nc, CHUNK, H, DH).transpose(0, 3, 1, 2, 4)

    q = chunks(q)
    k = chunks(k) * (DH ** -0.5)
    v = chunks(v)
    log_i = i_pre.astype(jnp.float32).reshape(B, nc, CHUNK, H).transpose(0, 3, 1, 2)
    log_f = jax.nn.log_sigmoid(f_pre.astype(jnp.float32)).reshape(B, nc, CHUNK, H).transpose(0, 3, 1, 2)
    b = jnp.cumsum(log_f, axis=-1)
    b_last = b[..., -1]

    g = b_last[..., None] - b + log_i
    m_loc = jnp.max(g, axis=-1)
    w_loc = jnp.exp(g - m_loc[..., None])
    C_loc = jnp.einsum('bhcs,bhcsv,bhcsk->bhcvk', w_loc, v, k)
    n_loc = jnp.einsum('bhcs,bhcsk->bhck', w_loc, k)

    def step(carry, inp):
        C, n, m = carry
        Cl, nl, ml, bl = inp
        m_new = jnp.maximum(bl + m, ml)
        a = jnp.exp(bl + m - m_new)
        c = jnp.exp(ml - m_new)
        C_new = a[..., None, None] * C + c[..., None, None] * Cl
        n_new = a[..., None] * n + c[..., None] * nl
        return (C_new, n_new, m_new), (C, n, m)

    init = (jnp.zeros((B, H, DH, DH), jnp.float32),
            jnp.zeros((B, H, DH), jnp.float32),
            jnp.zeros((B, H), jnp.float32))
    xs = (jnp.moveaxis(C_loc, 2, 0), jnp.moveaxis(n_loc, 2, 0),
          jnp.moveaxis(m_loc, 2, 0), jnp.moveaxis(b_last, 2, 0))
    _, (C_prev, n_prev, m_prev) = lax.scan(step, init, xs)
    C_prev = jnp.moveaxis(C_prev, 0, 2)
    n_prev = jnp.moveaxis(n_prev, 0, 2)
    m_prev = jnp.moveaxis(m_prev, 0, 2)

    causal = jnp.tril(jnp.ones((CHUNK, CHUNK), dtype=bool))
    D = jnp.where(causal, b[..., :, None] - b[..., None, :] + log_i[..., None, :], -jnp.inf)
    a_inter = b + m_prev[..., None]
    m = jnp.maximum(a_inter, jnp.max(D, axis=-1))
    W = jnp.exp(D - m[..., None]) * jnp.einsum('bhctd,bhcsd->bhcts', q, k)
    inter = jnp.exp(a_inter - m)
    num = (jnp.einsum('bhcts,bhcsv->bhctv', W, v)
           + inter[..., None] * jnp.einsum('bhcvk,bhctk->bhctv', C_prev, q))
    den = jnp.sum(W, axis=-1) + inter * jnp.einsum('bhck,bhctk->bhct', n_prev, q)
    h = num / jnp.maximum(jnp.abs(den), jnp.exp(-m))[..., None]
    return h.transpose(0, 2, 3, 1, 4).reshape(B, S, H, DH).astype(out_dtype)


def spatial_gating(u, vg, ln_gain, w_spatial, b_spatial, out_gain):
    B, S, _ = u.shape
    nc = S // CHUNK
    u = jax.nn.gelu(u)
    vg = layer_norm(jax.nn.gelu(vg), ln_gain)
    vc = vg.reshape(B, nc, CHUNK, G_GROUPS, G_DIM)
    causal = jnp.tril(jnp.ones((CHUNK, CHUNK), dtype=bool))
    w = jnp.where(causal[None], w_spatial, 0.0)
    mixed = jnp.einsum('gts,bcsgd->bctgd', w, vc) + b_spatial.T[:, :, None]
    return rms_norm(u * mixed.reshape(B, S, D_GMLP), out_gain)


def hybrid_mixer(xn, w_in, conv_qk, b_igate, b_fgate, mlstm_norm_gain,
                 gmlp_ln_gain, w_spatial, b_spatial, gmlp_out_gain, w_out):
    B, S, _ = xn.shape
    proj = xn @ w_in
    splits = np.cumsum(PROJ_SIZES)[:-1].tolist()
    q, k, v, o, ig, fg, u, vg = jnp.split(proj, splits, axis=-1)
    qk = jax.nn.silu(causal_dwconv(jnp.concatenate([q, k], axis=-1), conv_qk))
    q, k = jnp.split(qk, 2, axis=-1)
    heads = lambda t: t.reshape(B, S, M_HEADS, M_HEAD_DIM)
    h = mlstm_chunkwise(heads(q), heads(k), heads(v), ig + b_igate, fg + b_fgate)
    h = layer_norm(h, mlstm_norm_gain)
    h_m = jax.nn.sigmoid(o) * h.reshape(B, S, D_MLSTM)
    h_g = spatial_gating(u, vg, gmlp_ln_gain, w_spatial, b_spatial, gmlp_out_gain)
    return jnp.concatenate([h_m, h_g], axis=-1) @ w_out


def moe_ffn(x, w_router, b_router, w_up, b_up, w_down, b_down):
    T, D = x.shape
    logits = (x @ w_router + b_router).astype(jnp.float32)
    top_logit, top_idx = lax.top_k(logits, TOP_K)
    gate = jax.nn.softmax(top_logit, axis=-1).astype(x.dtype)
    A = T * TOP_K
    flat_e = top_idx.reshape(A)
    flat_tok = jnp.arange(A, dtype=jnp.int32) // TOP_K
    flat_g = gate.reshape(A)
    order = jnp.argsort(flat_e)
    se, stok, sg = flat_e[order], flat_tok[order], flat_g[order]
    counts = jnp.bincount(flat_e, length=N_EXPERTS)
    padded = (counts + MOE_BLOCK - 1) // MOE_BLOCK * MOE_BLOCK
    start = jnp.cumsum(counts) - counts
    pend = jnp.cumsum(padded)
    pstart = pend - padded
    dest = pstart[se] + (jnp.arange(A) - start[se])
    n_blocks = -(-A // MOE_BLOCK) + N_EXPERTS
    nbuf = n_blocks * MOE_BLOCK
    tok_buf = jnp.zeros((nbuf,), jnp.int32).at[dest].set(stok)
    g_buf = jnp.zeros((nbuf,), x.dtype).at[dest].set(sg)
    block_e = jnp.minimum(jnp.searchsorted(pend, jnp.arange(n_blocks) * MOE_BLOCK, side='right'),
                          N_EXPERTS - 1)

    def block_fn(args):
        toks, g, e = args
        xb = x[toks]
        hb = xb @ w_up[e] + b_up[e]
        h_glu, h_lin = jnp.split(hb, 2, axis=-1)
        h_glu = jnp.minimum(h_glu, SWIGLU_LIMIT)
        h_lin = jnp.clip(h_lin, -SWIGLU_LIMIT, SWIGLU_LIMIT)
        act = h_glu * jax.nn.sigmoid(SWIGLU_ALPHA * h_glu) * (h_lin + 1.0)
        return (act @ w_down[e] + b_down[e]) * g[:, None]

    yb = lax.map(block_fn, (tok_buf.reshape(n_blocks, MOE_BLOCK),
                            g_buf.reshape(n_blocks, MOE_BLOCK), block_e))
    return jax.ops.segment_sum(yb.reshape(nbuf, D), tok_buf, num_segments=T)


def setup_inputs(seed: int = 0) -> dict:
    key = jax.random.key(seed)
    ks = jax.random.split(key, 24)
    nrm = lambda k, shape, s: jax.random.normal(k, shape, jnp.float32) * s
    L = DEPTH
    return {
        "x": nrm(ks[0], (BATCH, SEQ, D_MODEL), 1.0),
        "norm1_gain": 1.0 + nrm(ks[1], (L, D_MODEL), 0.05),
        "w_in": nrm(ks[2], (L, D_MODEL, D_PROJ), D_MODEL ** -0.5),
        "conv_qk": nrm(ks[3], (L, CONV_K, 2 * D_MLSTM), CONV_K ** -0.5),
        "b_igate": nrm(ks[4], (L, M_HEADS), 0.1),
        "b_fgate": jnp.linspace(3.0, 6.0, M_HEADS, dtype=jnp.float32)[None] + nrm(ks[5], (L, M_HEADS), 0.1),
        "mlstm_norm_gain": 1.0 + nrm(ks[6], (L, M_HEADS, M_HEAD_DIM), 0.05),
        "gmlp_ln_gain": 1.0 + nrm(ks[7], (L, D_GMLP), 0.05),
        "w_spatial": nrm(ks[8], (L, G_GROUPS, CHUNK, CHUNK), CHUNK ** -0.5),
        "b_spatial": 1.0 + nrm(ks[9], (L, G_GROUPS, CHUNK), 0.1),
        "gmlp_out_gain": 1.0 + nrm(ks[10], (L, D_GMLP), 0.05),
        "w_out": nrm(ks[11], (L, D_MIX, D_MODEL), D_MIX ** -0.5),
        "norm2_gain": 1.0 + nrm(ks[12], (L, D_MODEL), 0.05),
        "w_router": nrm(ks[13], (L, D_MODEL, N_EXPERTS), D_MODEL ** -0.5),
        "b_router": nrm(ks[14], (L, N_EXPERTS), 0.01),
        "w_up": nrm(ks[15], (L, N_EXPERTS, D_MODEL, 2 * D_FF), D_MODEL ** -0.5),
        "b_up": nrm(ks[16], (L, N_EXPERTS, 2 * D_FF), 0.02),
        "w_down": nrm(ks[17], (L, N_EXPERTS, D_FF, D_MODEL), D_FF ** -0.5),
        "b_down": nrm(ks[18], (L, N_EXPERTS, D_MODEL), 0.02),
        "final_gain": 1.0 + nrm(ks[19], (D_MODEL,), 0.05),
    }


def reference(x, norm1_gain, w_in, conv_qk, b_igate, b_fgate, mlstm_norm_gain, gmlp_ln_gain,
              w_spatial, b_spatial, gmlp_out_gain, w_out, norm2_gain, w_router, b_router,
              w_up, b_up, w_down, b_down, final_gain):
    B, S, D = x.shape
    h = x
    for l in range(DEPTH):
        xn = rms_norm(h, norm1_gain[l])
        h = h + hybrid_mixer(xn, w_in[l], conv_qk[l], b_igate[l], b_fgate[l], mlstm_norm_gain[l],
                             gmlp_ln_gain[l], w_spatial[l], b_spatial[l], gmlp_out_gain[l], w_out[l])
        hn = rms_norm(h, norm2_gain[l])
        h = h + moe_ffn(hn.reshape(B * S, D), w_router[l], b_router[l], w_up[l], b_up[l],
                        w_down[l], b_down[l]).reshape(B, S, D)
    return rms_norm(h, final_gain)
```

```python
import functools

import jax
import jax.numpy as jnp
from jax import lax
from jax.experimental import pallas as pl
from jax.experimental.pallas import tpu as pltpu

F32 = jnp.float32
BF16 = jnp.bfloat16

D_MODEL = 1024
M_HEADS = 4
M_HEAD_DIM = 128
D_MLSTM = M_HEADS * M_HEAD_DIM
D_GMLP = D_MODEL - D_MLSTM
G_GROUPS = 4
G_DIM = D_GMLP // G_GROUPS
CHUNK = 128
CONV_K = 4
N_EXPERTS = 32
TOP_K = 4
D_FF = D_MODEL
SWIGLU_LIMIT = 7.0
SWIGLU_ALPHA = 1.702
EPS = 1e-6

LANES = 128
SUBLANES = 8
D_MAIN = 4 * D_MLSTM + 2 * D_GMLP
TM_PROJ = 512
TG_GMLP = 512
TM_ROUTE = 512
TD_DISPATCH = 512
TM_COMBINE = 256
BM_FFN = 256
VMEM_LIMIT = 56 * 1024 * 1024


def _cparams(n_axes=1):
    return pltpu.CompilerParams(
        dimension_semantics=("arbitrary",) * n_axes, vmem_limit_bytes=VMEM_LIMIT)


def _sigmoid(x):
    return 1.0 / (1.0 + jnp.exp(-x))


def _gelu_tanh(x):
    c = 0.7978845608028654
    return x * (0.5 * (1.0 + jnp.tanh(c * (x + 0.044715 * (x * x * x)))))


def _log_sigmoid(x):
    return jnp.minimum(x, 0.0) - jnp.log1p(jnp.exp(-jnp.abs(x)))


def _inproj_kernel(x_ref, gain_ref, wm_ref, wg_ref, main_ref, gate_ref):
    x = x_ref[...]
    xn = x * lax.rsqrt(jnp.mean(x * x, axis=-1, keepdims=True) + EPS) * gain_ref[...]
    xb = xn.astype(BF16)
    main_ref[...] = jnp.dot(xb, wm_ref[...], preferred_element_type=F32).astype(BF16)
    gate_ref[...] = jnp.dot(xb, wg_ref[...], preferred_element_type=F32)


def _inproj(x2, gain, w_main, w_gate):
    T = x2.shape[0]
    return pl.pallas_call(
        _inproj_kernel,
        grid=(T // TM_PROJ,),
        in_specs=[
            pl.BlockSpec((TM_PROJ, D_MODEL), lambda i: (i, 0)),
            pl.BlockSpec((1, D_MODEL), lambda i: (0, 0)),
            pl.BlockSpec((D_MODEL, D_MAIN), lambda i: (0, 0)),
            pl.BlockSpec((D_MODEL, LANES), lambda i: (0, 0)),
        ],
        out_specs=[
            pl.BlockSpec((TM_PROJ, D_MAIN), lambda i: (i, 0)),
            pl.BlockSpec((TM_PROJ, LANES), lambda i: (i, 0)),
        ],
        out_shape=[
            jax.ShapeDtypeStruct((T, D_MAIN), BF16),
            jax.ShapeDtypeStruct((T, LANES), F32),
        ],
        compiler_params=_cparams(),
        name="inproj",
    )(x2, gain, w_main, w_gate)


def _mlstm_kernel(qkvo_ref, gates_ref, conv_ref, gbias_ref, ngain_ref, out_ref,
                  xp_ref, ct_ref, n_ref, m_ref, *, batch):
    c = pl.program_id(0)
    L, DH = CHUNK, M_HEAD_DIM

    @pl.when(c == 0)
    def _():
        xp_ref[:, 0:SUBLANES, :] = jnp.zeros((batch, SUBLANES, 2 * D_MLSTM), F32)
        ct_ref[...] = jnp.zeros_like(ct_ref)
        n_ref[...] = jnp.zeros_like(n_ref)
        m_ref[...] = jnp.zeros_like(m_ref)

    row = lax.broadcasted_iota(jnp.int32, (L, L), 0)
    col = lax.broadcasted_iota(jnp.int32, (L, L), 1)
    tril = row >= col
    tril_f = tril.astype(F32)

    for b in range(batch):
        xp_ref[b, SUBLANES:SUBLANES + L, :] = qkvo_ref[b, :, 0:2 * D_MLSTM].astype(F32)
        off = SUBLANES - (CONV_K - 1)
        acc = xp_ref[b, off:off + L, :] * conv_ref[0:1, :]
        for j in range(1, CONV_K):
            acc = acc + xp_ref[b, off + j:off + j + L, :] * conv_ref[j:j + 1, :]
        xp_ref[b, 0:SUBLANES, :] = xp_ref[b, L:L + SUBLANES, :]
        qk = acc * _sigmoid(acc)

        gc = gates_ref[b] + gbias_ref[...]
        lf = _log_sigmoid(gc)
        ball = jnp.dot(tril_f, lf, precision=lax.Precision.HIGHEST,
                       preferred_element_type=F32)
        gc_t = gc.T
        ball_t = ball.T

        for h in range(M_HEADS):
            s = b * M_HEADS + h
            q = qk[:, h * DH:(h + 1) * DH]
            k = qk[:, D_MLSTM + h * DH:D_MLSTM + (h + 1) * DH] * (DH ** -0.5)
            v = qkvo_ref[b, :, 2 * D_MLSTM + h * DH:2 * D_MLSTM + (h + 1) * DH]
            o = qkvo_ref[b, :, 3 * D_MLSTM + h * DH:3 * D_MLSTM + (h + 1) * DH].astype(F32)
            qb = q.astype(BF16)
            kb = k.astype(BF16)

            b_col = ball[:, M_HEADS + h:M_HEADS + h + 1]
            b_row = ball_t[M_HEADS + h:M_HEADS + h + 1, :]
            li_col = gc[:, h:h + 1]
            li_row = gc_t[h:h + 1, :]
            b_last = b_col[L - 1:L, :]
            m_prev = m_ref[s][0:1, 0:1]
            ct_prev = ct_ref[s]
            n_prev = n_ref[s]

            dmat = jnp.where(tril, b_col - b_row + li_row, -jnp.inf)
            a_inter = b_col + m_prev
            m_t = jnp.maximum(a_inter, jnp.max(dmat, axis=-1, keepdims=True))
            qkt = lax.dot_general(qb, kb, (((1,), (1,)), ((), ())), preferred_element_type=F32)
            w = jnp.exp(dmat - m_t) * qkt
            inter = jnp.exp(a_inter - m_t)
            num = (jnp.dot(w.astype(BF16), v, preferred_element_type=F32)
                   + inter * jnp.dot(qb, ct_prev.astype(BF16), preferred_element_type=F32))
            den = (jnp.sum(w, axis=-1, keepdims=True)
                   + inter * jnp.sum(q * n_prev, axis=-1, keepdims=True))
            hh = num * (1.0 / jnp.maximum(jnp.abs(den), jnp.exp(-m_t)))

            mu = jnp.mean(hh, axis=-1, keepdims=True)
            xc = hh - mu
            y = xc * lax.rsqrt(jnp.mean(xc * xc, axis=-1, keepdims=True) + EPS)
            y = y * ngain_ref[:, h * DH:(h + 1) * DH]
            out_ref[b, :, h * DH:(h + 1) * DH] = (_sigmoid(o) * y).astype(out_ref.dtype)

            g_col = b_last - b_col + li_col
            m_loc = jnp.max(g_col, axis=0, keepdims=True)
            w_col = jnp.exp(g_col - m_loc)
            wv = (w_col * v.astype(F32)).astype(BF16)
            ct_loc = jnp.dot(k.T.astype(BF16), wv, preferred_element_type=F32)
            n_loc = jnp.sum(w_col * k, axis=0, keepdims=True)
            m_new = jnp.maximum(b_last + m_prev, m_loc)
            a = jnp.exp(b_last + m_prev - m_new)
            cc = jnp.exp(m_loc - m_new)
            ct_ref[s] = a * ct_prev + cc * ct_loc
            n_ref[s] = a * n_prev + cc * n_loc
            m_ref[s] = jnp.broadcast_to(m_new, (SUBLANES, LANES))


def _mlstm(main3, gates3, conv_qk, gbias, ngain):
    B, S, _ = main3.shape
    nc = S // CHUNK
    return pl.pallas_call(
        functools.partial(_mlstm_kernel, batch=B),
        grid=(nc,),
        in_specs=[
            pl.BlockSpec((B, CHUNK, 4 * D_MLSTM), lambda c: (0, c, 0)),
            pl.BlockSpec((B, CHUNK, LANES), lambda c: (0, c, 0)),
            pl.BlockSpec((CONV_K, 2 * D_MLSTM), lambda c: (0, 0)),
            pl.BlockSpec((1, LANES), lambda c: (0, 0)),
            pl.BlockSpec((1, D_MLSTM), lambda c: (0, 0)),
        ],
        out_specs=pl.BlockSpec((B, CHUNK, D_MLSTM), lambda c: (0, c, 0)),
        out_shape=jax.ShapeDtypeStruct((B, S, D_MLSTM), BF16),
        scratch_shapes=[
            pltpu.VMEM((B, CHUNK + SUBLANES, 2 * D_MLSTM), F32),
            pltpu.VMEM((B * M_HEADS, M_HEAD_DIM, M_HEAD_DIM), F32),
            pltpu.VMEM((B * M_HEADS, 1, M_HEAD_DIM), F32),
            pltpu.VMEM((B * M_HEADS, SUBLANES, LANES), F32),
        ],
        compiler_params=_cparams(),
        name="mlstm",
    )(main3, gates3, conv_qk, gbias, ngain)


def _gmlp_kernel(uv_ref, lng_ref, ws_ref, bs_ref, og_ref, out_ref, prod_ref):
    L = CHUNK
    u = _gelu_tanh(uv_ref[:, 0:D_GMLP].astype(F32))
    vg = _gelu_tanh(uv_ref[:, D_GMLP:2 * D_GMLP].astype(F32))
    mu = jnp.mean(vg, axis=-1, keepdims=True)
    xc = vg - mu
    vg = xc * lax.rsqrt(jnp.mean(xc * xc, axis=-1, keepdims=True) + EPS) * lng_ref[...]
    vb = vg.astype(BF16)
    row = lax.broadcasted_iota(jnp.int32, (L, L), 0)
    col = lax.broadcasted_iota(jnp.int32, (L, L), 1)
    tril = row >= col
    for g in range(G_GROUPS):
        wg = jnp.where(tril, ws_ref[g], 0.0).astype(BF16)
        b_col = bs_ref[:, g:g + 1]
        for j in range(TG_GMLP // L):
            mixed = jnp.dot(wg, vb[j * L:(j + 1) * L, g * G_DIM:(g + 1) * G_DIM],
                            preferred_element_type=F32) + b_col
            prod_ref[j * L:(j + 1) * L, g * G_DIM:(g + 1) * G_DIM] = (
                u[j * L:(j + 1) * L, g * G_DIM:(g + 1) * G_DIM] * mixed)
    p = prod_ref[...]
    y = p * lax.rsqrt(jnp.mean(p * p, axis=-1, keepdims=True) + EPS) * og_ref[...]
    out_ref[...] = y.astype(out_ref.dtype)


def _gmlp(main, ln_gain, w_spatial, b_spatial_t, out_gain):
    T = main.shape[0]
    uv_block = 4 * D_MLSTM // (2 * D_GMLP)
    return pl.pallas_call(
        _gmlp_kernel,
        grid=(T // TG_GMLP,),
        in_specs=[
            pl.BlockSpec((TG_GMLP, 2 * D_GMLP), lambda i: (i, uv_block)),
            pl.BlockSpec((1, D_GMLP), lambda i: (0, 0)),
            pl.BlockSpec((G_GROUPS, CHUNK, CHUNK), lambda i: (0, 0, 0)),
            pl.BlockSpec((CHUNK, G_GROUPS), lambda i: (0, 0)),
            pl.BlockSpec((1, D_GMLP), lambda i: (0, 0)),
        ],
        out_specs=pl.BlockSpec((TG_GMLP, D_GMLP), lambda i: (i, 0)),
        out_shape=jax.ShapeDtypeStruct((T, D_GMLP), BF16),
        scratch_shapes=[pltpu.VMEM((TG_GMLP, D_GMLP), F32)],
        compiler_params=_cparams(),
        name="gmlp",
    )(main, ln_gain, w_spatial, b_spatial_t, out_gain)


def _outproj_kernel(hm_ref, hg_ref, x_ref, wo_ref, g2_ref, wr_ref, br_ref, tril_ref,
                    h_ref, hn_ref, route_ref, cnt_ref, carry_ref):
    i = pl.program_id(0)
    tm = TM_ROUTE

    @pl.when(i == 0)
    def _():
        carry_ref[...] = jnp.zeros_like(carry_ref)

    y = (jnp.dot(hm_ref[...], wo_ref[0:D_MLSTM, :], preferred_element_type=F32)
         + jnp.dot(hg_ref[...], wo_ref[D_MLSTM:D_MODEL, :], preferred_element_type=F32))
    h = x_ref[...] + y
    h_ref[...] = h
    hn = h * lax.rsqrt(jnp.mean(h * h, axis=-1, keepdims=True) + EPS) * g2_ref[...]
    hn_ref[...] = hn

    logits = jnp.dot(hn.astype(BF16), wr_ref[...], preferred_element_type=F32) + br_ref[...]
    lane = lax.broadcasted_iota(jnp.int32, (tm, LANES), 1).astype(F32)
    l = jnp.where(lane < N_EXPERTS, logits, -jnp.inf)
    top_v, top_i = [], []
    for _ in range(TOP_K):
        mx = jnp.max(l, axis=-1, keepdims=True)
        ix = jnp.min(jnp.where(l == mx, lane, float(LANES)), axis=-1, keepdims=True)
        top_v.append(mx)
        top_i.append(ix)
        l = jnp.where(lane == ix, -jnp.inf, l)
    ex = [jnp.exp(v - top_v[0]) for v in top_v]
    inv = 1.0 / (ex[0] + ex[1] + ex[2] + ex[3])

    onehot = [lane == ix for ix in top_i]
    mask = (onehot[0].astype(F32) + onehot[1].astype(F32)
            + onehot[2].astype(F32) + onehot[3].astype(F32))
    incl = jnp.dot(tril_ref[...], mask.astype(BF16), preferred_element_type=F32)
    base = carry_ref[0:1, :] + incl - mask
    route = jnp.zeros((tm, LANES), F32)
    for k in range(TOP_K):
        rank = jnp.sum(jnp.where(onehot[k], base, 0.0), axis=-1, keepdims=True)
        route = jnp.where(lane == k, top_i[k], route)
        route = jnp.where(lane == TOP_K + k, rank, route)
        route = jnp.where(lane == 2 * TOP_K + k, ex[k] * inv, route)
    route_ref[...] = route
    carry_ref[...] = carry_ref[...] + incl[tm - 1:tm, :]
    cnt_ref[...] = carry_ref[...]


def _outproj(h_m, h_g, x2, w_out, gain2, w_router, b_router, tril):
    T = x2.shape[0]
    tm = TM_ROUTE
    return pl.pallas_call(
        _outproj_kernel,
        grid=(T // tm,),
        in_specs=[
            pl.BlockSpec((tm, D_MLSTM), lambda i: (i, 0)),
            pl.BlockSpec((tm, D_GMLP), lambda i: (i, 0)),
            pl.BlockSpec((tm, D_MODEL), lambda i: (i, 0)),
            pl.BlockSpec((D_MODEL, D_MODEL), lambda i: (0, 0)),
            pl.BlockSpec((1, D_MODEL), lambda i: (0, 0)),
            pl.BlockSpec((D_MODEL, LANES), lambda i: (0, 0)),
            pl.BlockSpec((1, LANES), lambda i: (0, 0)),
            pl.BlockSpec((tm, tm), lambda i: (0, 0)),
        ],
        out_specs=[
            pl.BlockSpec((tm, D_MODEL), lambda i: (i, 0)),
            pl.BlockSpec((tm, D_MODEL), lambda i: (i, 0)),
            pl.BlockSpec((tm, LANES), lambda i: (i, 0)),
            pl.BlockSpec((SUBLANES, LANES), lambda i: (0, 0)),
        ],
        out_shape=[
            jax.ShapeDtypeStruct((T, D_MODEL), F32),
            jax.ShapeDtypeStruct((T, D_MODEL), F32),
            jax.ShapeDtypeStruct((T, LANES), F32),
            jax.ShapeDtypeStruct((SUBLANES, LANES), F32),
        ],
        scratch_shapes=[pltpu.VMEM((SUBLANES, LANES), F32)],
        compiler_params=_cparams(),
        name="outproj_router",
    )(h_m, h_g, x2, w_out, gain2, w_router, b_router, tril)


def _dest_kernel(route_ref, pstart_ref, dest_ref):
    r = route_ref[...]
    tm = r.shape[0]
    lane = lax.broadcasted_iota(jnp.int32, (tm, LANES), 1).astype(F32)
    ps = pstart_ref[...]
    out = jnp.zeros((tm, LANES), F32)
    for k in range(TOP_K):
        ix = r[:, k:k + 1]
        rank = r[:, TOP_K + k:TOP_K + k + 1]
        d = rank + jnp.sum(jnp.where(lane == ix, ps, 0.0), axis=-1, keepdims=True)
        out = jnp.where(lane == k, d, out)
    dest_ref[...] = out.astype(jnp.int32)


def _dest(route, pstart):
    T = route.shape[0]
    tm = TM_ROUTE
    return pl.pallas_call(
        _dest_kernel,
        grid=(T // tm,),
        in_specs=[
            pl.BlockSpec((tm, LANES), lambda i: (i, 0)),
            pl.BlockSpec((1, LANES), lambda i: (0, 0)),
        ],
        out_specs=pl.BlockSpec((tm, LANES), lambda i: (i, 0)),
        out_shape=jax.ShapeDtypeStruct((T, LANES), jnp.int32),
        compiler_params=_cparams(),
        name="dest",
    )(route, pstart)


def _dispatch_kernel(dest_ref, hn_hbm, xs_in_hbm, xs_hbm, sem):
    del xs_in_hbm
    base = pl.program_id(0) * TD_DISPATCH

    def row_copy(tok, k):
        d = dest_ref[tok * TOP_K + k]
        return pltpu.make_async_copy(hn_hbm.at[pl.ds(tok, 1)], xs_hbm.at[pl.ds(d, 1)], sem)

    def issue(t, carry):
        for k in range(TOP_K):
            row_copy(base + t, k).start()
        return carry

    def drain(t, carry):
        for k in range(TOP_K):
            row_copy(base + t, k).wait()
        return carry

    lax.fori_loop(0, TD_DISPATCH, issue, 0)
    lax.fori_loop(0, TD_DISPATCH, drain, 0)


def _dispatch(dest_flat, hn, xs_init):
    T = hn.shape[0]
    return pl.pallas_call(
        _dispatch_kernel,
        grid_spec=pltpu.PrefetchScalarGridSpec(
            num_scalar_prefetch=1,
            grid=(T // TD_DISPATCH,),
            in_specs=[pl.BlockSpec(memory_space=pl.ANY), pl.BlockSpec(memory_space=pl.ANY)],
            out_specs=pl.BlockSpec(memory_space=pl.ANY),
            scratch_shapes=[pltpu.SemaphoreType.DMA],
        ),
        out_shape=jax.ShapeDtypeStruct(xs_init.shape, xs_init.dtype),
        input_output_aliases={2: 0},
        compiler_params=_cparams(),
        name="dispatch",
    )(dest_flat, hn, xs_init)


def _ffn_kernel(be_ref, nv_ref, xs_ref, wup_ref, bup_ref, wdn_ref, bdn_ref, y_ref,
                wup_bf, wdn_bf):
    i = pl.program_id(0)
    e = be_ref[i]
    e_prev = be_ref[jnp.maximum(i - 1, 0)]
    first = jnp.logical_or(i == 0, e != e_prev)
    valid = i < nv_ref[0]

    @pl.when(jnp.logical_and(first, valid))
    def _():
        wup_bf[...] = wup_ref[0].astype(BF16)
        wdn_bf[...] = wdn_ref[0].astype(BF16)

    @pl.when(valid)
    def _():
        xb = xs_ref[...].astype(BF16)
        hb = jnp.dot(xb, wup_bf[...], preferred_element_type=F32) + bup_ref[0]
        glu = jnp.minimum(hb[:, 0:D_FF], SWIGLU_LIMIT)
        lin = jnp.clip(hb[:, D_FF:2 * D_FF], -SWIGLU_LIMIT, SWIGLU_LIMIT)
        act = glu * _sigmoid(SWIGLU_ALPHA * glu) * (lin + 1.0)
        y_ref[...] = (jnp.dot(act.astype(BF16), wdn_bf[...], preferred_element_type=F32)
                      + bdn_ref[0])

    @pl.when(jnp.logical_not(valid))
    def _():
        y_ref[...] = jnp.zeros_like(y_ref)


def _ffn(block_e, n_valid, xs, w_up, b_up3, w_down, b_down3):
    nbuf = xs.shape[0]
    nb = nbuf // BM_FFN
    return pl.pallas_call(
        _ffn_kernel,
        grid_spec=pltpu.PrefetchScalarGridSpec(
            num_scalar_prefetch=2,
            grid=(nb,),
            in_specs=[
                pl.BlockSpec((BM_FFN, D_MODEL), lambda i, be, nv: (i, 0)),
                pl.BlockSpec((1, D_MODEL, 2 * D_FF), lambda i, be, nv: (be[i], 0, 0)),
                pl.BlockSpec((1, 1, 2 * D_FF), lambda i, be, nv: (be[i], 0, 0)),
                pl.BlockSpec((1, D_FF, D_MODEL), lambda i, be, nv: (be[i], 0, 0)),
                pl.BlockSpec((1, 1, D_MODEL), lambda i, be, nv: (be[i], 0, 0)),
            ],
            out_specs=pl.BlockSpec((BM_FFN, D_MODEL), lambda i, be, nv: (i, 0)),
            scratch_shapes=[
                pltpu.VMEM((D_MODEL, 2 * D_FF), BF16),
                pltpu.VMEM((D_FF, D_MODEL), BF16),
            ],
        ),
        out_shape=jax.ShapeDtypeStruct((nbuf, D_MODEL), F32),
        compiler_params=_cparams(),
        name="expert_ffn",
    )(block_e, n_valid, xs, w_up, b_up3, w_down, b_down3)


def _combine_kernel(dest_ref, route_ref, h_ref, gain_ref, yb_hbm, out_ref, buf_ref, sem):
    tm = TM_COMBINE
    base = pl.program_id(0) * tm

    def row_copy(t, k):
        d = dest_ref[(base + t) * TOP_K + k]
        return pltpu.make_async_copy(yb_hbm.at[pl.ds(d, 1)], buf_ref.at[k, pl.ds(t, 1)], sem)

    def issue(t, carry):
        for k in range(TOP_K):
            row_copy(t, k).start()
        return carry

    def drain(t, carry):
        for k in range(TOP_K):
            row_copy(t, k).wait()
        return carry

    lax.fori_loop(0, tm, issue, 0)
    lax.fori_loop(0, tm, drain, 0)

    r = route_ref[...]
    acc = h_ref[...]
    for k in range(TOP_K):
        acc = acc + r[:, 2 * TOP_K + k:2 * TOP_K + k + 1] * buf_ref[k]
    out = acc * lax.rsqrt(jnp.mean(acc * acc, axis=-1, keepdims=True) + EPS) * gain_ref[...]
    out_ref[...] = out


def _combine(dest_flat, route, h, gain, yb):
    T = h.shape[0]
    tm = TM_COMBINE
    return pl.pallas_call(
        _combine_kernel,
        grid_spec=pltpu.PrefetchScalarGridSpec(
            num_scalar_prefetch=1,
            grid=(T // tm,),
            in_specs=[
                pl.BlockSpec((tm, LANES), lambda i, d: (i, 0)),
                pl.BlockSpec((tm, D_MODEL), lambda i, d: (i, 0)),
                pl.BlockSpec((1, D_MODEL), lambda i, d: (0, 0)),
                pl.BlockSpec(memory_space=pl.ANY),
            ],
            out_specs=pl.BlockSpec((tm, D_MODEL), lambda i, d: (i, 0)),
            scratch_shapes=[
                pltpu.VMEM((TOP_K, tm, D_MODEL), F32),
                pltpu.SemaphoreType.DMA,
            ],
        ),
        out_shape=jax.ShapeDtypeStruct((T, D_MODEL), F32),
        compiler_params=_cparams(),
        name="combine",
    )(dest_flat, route, h, gain, yb)


def _layer(x2, B, S, norm1_gain, w_in, conv_qk, b_igate, b_fgate, mlstm_norm_gain, gmlp_ln_gain,
           w_spatial, b_spatial, gmlp_out_gain, w_out, norm2_gain, w_router, b_router,
           w_up, b_up, w_down, b_down):
    T = B * S
    n_gate = 2 * M_HEADS
    g0 = 4 * D_MLSTM
    w_main = jnp.concatenate([w_in[:, :g0], w_in[:, g0 + n_gate:]], axis=1).astype(BF16)
    w_gate = jnp.pad(w_in[:, g0:g0 + n_gate], ((0, 0), (0, LANES - n_gate))).astype(BF16)
    gbias = jnp.pad(jnp.concatenate([b_igate, b_fgate]), (0, LANES - n_gate)).reshape(1, LANES)

    main, gates = _inproj(x2, norm1_gain.reshape(1, D_MODEL), w_main, w_gate)
    h_m = _mlstm(main.reshape(B, S, D_MAIN), gates.reshape(B, S, LANES), conv_qk, gbias,
                 mlstm_norm_gain.reshape(1, D_MLSTM)).reshape(T, D_MLSTM)
    h_g = _gmlp(main, gmlp_ln_gain.reshape(1, D_GMLP), w_spatial, b_spatial.T,
                gmlp_out_gain.reshape(1, D_GMLP))

    w_r = jnp.pad(w_router, ((0, 0), (0, LANES - N_EXPERTS))).astype(BF16)
    b_r = jnp.pad(b_router, (0, LANES - N_EXPERTS)).reshape(1, LANES)
    tril = jnp.tril(jnp.ones((TM_ROUTE, TM_ROUTE), BF16))
    h, hn, route, cnt = _outproj(h_m, h_g, x2, w_out.astype(BF16),
                                 norm2_gain.reshape(1, D_MODEL), w_r, b_r, tril)

    counts = cnt[0, :N_EXPERTS].astype(jnp.int32)
    padded = (counts + BM_FFN - 1) // BM_FFN * BM_FFN
    pend = jnp.cumsum(padded)
    pstart = pend - padded
    nb = (T * TOP_K) // BM_FFN + N_EXPERTS
    n_valid = (pend[-1] // BM_FFN).astype(jnp.int32)
    blk = jnp.arange(nb, dtype=jnp.int32)
    block_e = jnp.minimum(jnp.searchsorted(pend, blk * BM_FFN, side='right'),
                          N_EXPERTS - 1).astype(jnp.int32)
    block_e = jnp.where(blk < n_valid, block_e, block_e[jnp.maximum(n_valid - 1, 0)])
    pstart_row = jnp.pad(pstart.astype(F32), (0, LANES - N_EXPERTS)).reshape(1, LANES)

    dest = _dest(route, pstart_row)
    dest_flat = dest[:, :TOP_K].reshape(T * TOP_K)

    xs = _dispatch(dest_flat, hn, jnp.zeros((nb * BM_FFN, D_MODEL), F32))
    yb = _ffn(block_e, n_valid.reshape(1), xs, w_up, b_up.reshape(N_EXPERTS, 1, 2 * D_FF),
              w_down, b_down.reshape(N_EXPERTS, 1, D_MODEL))
    return dest_flat, route, h, yb


def kernel(x, norm1_gain, w_in, conv_qk, b_igate, b_fgate, mlstm_norm_gain, gmlp_ln_gain,
           w_spatial, b_spatial, gmlp_out_gain, w_out, norm2_gain, w_router, b_router,
           w_up, b_up, w_down, b_down, final_gain):
    B, S, D = x.shape
    depth = norm1_gain.shape[0]
    assert depth == 1 and D == D_MODEL and S % CHUNK == 0
    x2 = x.reshape(B * S, D)
    l = 0
    dest_flat, route, h, yb = _layer(
        x2, B, S, norm1_gain[l], w_in[l], conv_qk[l], b_igate[l], b_fgate[l], mlstm_norm_gain[l],
        gmlp_ln_gain[l], w_spatial[l], b_spatial[l], gmlp_out_gain[l], w_out[l], norm2_gain[l],
        w_router[l], b_router[l], w_up[l], b_up[l], w_down[l], b_down[l])
    out = _combine(dest_flat, route, h, final_gain.reshape(1, D_MODEL), yb)
    return out.reshape(B, S, D)
```

```python
import functools

import jax
import jax.numpy as jnp
from jax import lax
from jax.experimental import pallas as pl
from jax.experimental.pallas import tpu as pltpu

F32 = jnp.float32
BF16 = jnp.bfloat16

D_MODEL = 1024
M_HEADS = 4
M_HEAD_DIM = 128
D_MLSTM = M_HEADS * M_HEAD_DIM
D_GMLP = D_MODEL - D_MLSTM
G_GROUPS = 4
G_DIM = D_GMLP // G_GROUPS
CHUNK = 128
CONV_K = 4
N_EXPERTS = 32
TOP_K = 4
D_FF = D_MODEL
SWIGLU_LIMIT = 7.0
SWIGLU_ALPHA = 1.702
EPS = 1e-6

LANES = 128
SUBLANES = 8
D_MAIN = 4 * D_MLSTM + 2 * D_GMLP
TM_PROJ = 512
TG_GMLP = 512
TM_ROUTE = 512
TD_DISPATCH = 512
TM_COMBINE = 256
BM_FFN = 256
VMEM_LIMIT = 56 * 1024 * 1024


def _cparams(n_axes=1):
    return pltpu.CompilerParams(
        dimension_semantics=("arbitrary",) * n_axes, vmem_limit_bytes=VMEM_LIMIT)


def _sigmoid(x):
    return 1.0 / (1.0 + jnp.exp(-x))


def _gelu_tanh(x):
    c = 0.7978845608028654
    return x * (0.5 * (1.0 + jnp.tanh(c * (x + 0.044715 * (x * x * x)))))


def _log_sigmoid(x):
    return jnp.minimum(x, 0.0) - jnp.log1p(jnp.exp(-jnp.abs(x)))


def _inproj_kernel(x_ref, gain_ref, wm_ref, wg_ref, main_ref, gate_ref):
    x = x_ref[...]
    xn = x * lax.rsqrt(jnp.mean(x * x, axis=-1, keepdims=True) + EPS) * gain_ref[...]
    xb = xn.astype(BF16)
    main_ref[...] = jnp.dot(xb, wm_ref[...], preferred_element_type=F32).astype(BF16)
    gate_ref[...] = jnp.dot(xb, wg_ref[...], preferred_element_type=F32)


def _inproj(x2, gain, w_main, w_gate):
    T = x2.shape[0]
    return pl.pallas_call(
        _inproj_kernel,
        grid=(T // TM_PROJ,),
        in_specs=[
            pl.BlockSpec((TM_PROJ, D_MODEL), lambda i: (i, 0)),
            pl.BlockSpec((1, D_MODEL), lambda i: (0, 0)),
            pl.BlockSpec((D_MODEL, D_MAIN), lambda i: (0, 0)),
            pl.BlockSpec((D_MODEL, LANES), lambda i: (0, 0)),
        ],
        out_specs=[
            pl.BlockSpec((TM_PROJ, D_MAIN), lambda i: (i, 0)),
            pl.BlockSpec((TM_PROJ, LANES), lambda i: (i, 0)),
        ],
        out_shape=[
            jax.ShapeDtypeStruct((T, D_MAIN), BF16),
            jax.ShapeDtypeStruct((T, LANES), F32),
        ],
        compiler_params=_cparams(),
        name="inproj",
    )(x2, gain, w_main, w_gate)


def _mlstm_kernel(qkvo_ref, gates_ref, conv_ref, gbias_ref, ngain_ref, out_ref,
                  xp_ref, ct_ref, n_ref, m_ref, *, batch):
    c = pl.program_id(0)
    L, DH = CHUNK, M_HEAD_DIM

    @pl.when(c == 0)
    def _():
        xp_ref[:, 0:SUBLANES, :] = jnp.zeros((batch, SUBLANES, 2 * D_MLSTM), F32)
        ct_ref[...] = jnp.zeros_like(ct_ref)
        n_ref[...] = jnp.zeros_like(n_ref)
        m_ref[...] = jnp.zeros_like(m_ref)

    row = lax.broadcasted_iota(jnp.int32, (L, L), 0)
    col = lax.broadcasted_iota(jnp.int32, (L, L), 1)
    tril = row >= col
    tril_f = tril.astype(F32)

    for b in range(batch):
        xp_ref[b, SUBLANES:SUBLANES + L, :] = qkvo_ref[b, :, 0:2 * D_MLSTM].astype(F32)
        off = SUBLANES - (CONV_K - 1)
        acc = xp_ref[b, off:off + L, :] * conv_ref[0:1, :]
        for j in range(1, CONV_K):
            acc = acc + xp_ref[b, off + j:off + j + L, :] * conv_ref[j:j + 1, :]
        xp_ref[b, 0:SUBLANES, :] = xp_ref[b, L:L + SUBLANES, :]
        qk = acc * _sigmoid(acc)

        gc = gates_ref[b] + gbias_ref[...]
        lf = _log_sigmoid(gc)
        ball = jnp.dot(tril_f, lf, precision=lax.Precision.HIGHEST,
                       preferred_element_type=F32)
        gc_t = gc.T
        ball_t = ball.T

        for h in range(M_HEADS):
            s = b * M_HEADS + h
            q = qk[:, h * DH:(h + 1) * DH]
            k = qk[:, D_MLSTM + h * DH:D_MLSTM + (h + 1) * DH] * (DH ** -0.5)
            v = qkvo_ref[b, :, 2 * D_MLSTM + h * DH:2 * D_MLSTM + (h + 1) * DH]
            o = qkvo_ref[b, :, 3 * D_MLSTM + h * DH:3 * D_MLSTM + (h + 1) * DH].astype(F32)
            qb = q.astype(BF16)
            kb = k.astype(BF16)

            b_col = ball[:, M_HEADS + h:M_HEADS + h + 1]
            b_row = ball_t[M_HEADS + h:M_HEADS + h + 1, :]
            li_col = gc[:, h:h + 1]
            li_row = gc_t[h:h + 1, :]
            b_last = b_col[L - 1:L, :]
            m_prev = m_ref[s][0:1, 0:1]
            ct_prev = ct_ref[s]
            n_prev = n_ref[s]

            dmat = jnp.where(tril, b_col - b_row + li_row, -jnp.inf)
            a_inter = b_col + m_prev
            m_t = jnp.maximum(a_inter, jnp.max(dmat, axis=-1, keepdims=True))
            qkt = lax.dot_general(qb, kb, (((1,), (1,)), ((), ())), preferred_element_type=F32)
            w = jnp.exp(dmat - m_t) * qkt
            inter = jnp.exp(a_inter - m_t)
            num = (jnp.dot(w.astype(BF16), v, preferred_element_type=F32)
                   + inter * jnp.dot(qb, ct_prev.astype(BF16), preferred_element_type=F32))
            den = (jnp.sum(w, axis=-1, keepdims=True)
                   + inter * jnp.sum(q * n_prev, axis=-1, keepdims=True))
            hh = num * (1.0 / jnp.maximum(jnp.abs(den), jnp.exp(-m_t)))

            mu = jnp.mean(hh, axis=-1, keepdims=True)
            xc = hh - mu
            y = xc * lax.rsqrt(jnp.mean(xc * xc, axis=-1, keepdims=True) + EPS)
            y = y * ngain_ref[:, h * DH:(h + 1) * DH]
            out_ref[b, :, h * DH:(h + 1) * DH] = (_sigmoid(o) * y).astype(out_ref.dtype)

            g_col = b_last - b_col + li_col
            m_loc = jnp.max(g_col, axis=0, keepdims=True)
            w_col = jnp.exp(g_col - m_loc)
            wv = (w_col * v.astype(F32)).astype(BF16)
            ct_loc = jnp.dot(k.T.astype(BF16), wv, preferred_element_type=F32)
            n_loc = jnp.sum(w_col * k, axis=0, keepdims=True)
            m_new = jnp.maximum(b_last + m_prev, m_loc)
            a = jnp.exp(b_last + m_prev - m_new)
            cc = jnp.exp(m_loc - m_new)
            ct_ref[s] = a * ct_prev + cc * ct_loc
            n_ref[s] = a * n_prev + cc * n_loc
            m_ref[s] = jnp.broadcast_to(m_new, (SUBLANES, LANES))


def _mlstm(main3, gates3, conv_qk, gbias, ngain):
    B, S, _ = main3.shape
    nc = S // CHUNK
    return pl.pallas_call(
        functools.partial(_mlstm_kernel, batch=B),
        grid=(nc,),
        in_specs=[
            pl.BlockSpec((B, CHUNK, 4 * D_MLSTM), lambda c: (0, c, 0)),
            pl.BlockSpec((B, CHUNK, LANES), lambda c: (0, c, 0)),
            pl.BlockSpec((CONV_K, 2 * D_MLSTM), lambda c: (0, 0)),
            pl.BlockSpec((1, LANES), lambda c: (0, 0)),
            pl.BlockSpec((1, D_MLSTM), lambda c: (0, 0)),
        ],
        out_specs=pl.BlockSpec((B, CHUNK, D_MLSTM), lambda c: (0, c, 0)),
        out_shape=jax.ShapeDtypeStruct((B, S, D_MLSTM), BF16),
        scratch_shapes=[
            pltpu.VMEM((B, CHUNK + SUBLANES, 2 * D_MLSTM), F32),
            pltpu.VMEM((B * M_HEADS, M_HEAD_DIM, M_HEAD_DIM), F32),
            pltpu.VMEM((B * M_HEADS, 1, M_HEAD_DIM), F32),
            pltpu.VMEM((B * M_HEADS, SUBLANES, LANES), F32),
        ],
        compiler_params=_cparams(),
        name="mlstm",
    )(main3, gates3, conv_qk, gbias, ngain)


def _gmlp_kernel(uv_ref, lng_ref, ws_ref, bs_ref, og_ref, out_ref, prod_ref):
    L = CHUNK
    u = _gelu_tanh(uv_ref[:, 0:D_GMLP].astype(F32))
    vg = _gelu_tanh(uv_ref[:, D_GMLP:2 * D_GMLP].astype(F32))
    mu = jnp.mean(vg, axis=-1, keepdims=True)
    xc = vg - mu
    vg = xc * lax.rsqrt(jnp.mean(xc * xc, axis=-1, keepdims=True) + EPS) * lng_ref[...]
    vb = vg.astype(BF16)
    row = lax.broadcasted_iota(jnp.int32, (L, L), 0)
    col = lax.broadcasted_iota(jnp.int32, (L, L), 1)
    tril = row >= col
    for g in range(G_GROUPS):
        wg = jnp.where(tril, ws_ref[g], 0.0).astype(BF16)
        b_col = bs_ref[:, g:g + 1]
        for j in range(TG_GMLP // L):
            mixed = jnp.dot(wg, vb[j * L:(j + 1) * L, g * G_DIM:(g + 1) * G_DIM],
                            preferred_element_type=F32) + b_col
            prod_ref[j * L:(j + 1) * L, g * G_DIM:(g + 1) * G_DIM] = (
                u[j * L:(j + 1) * L, g * G_DIM:(g + 1) * G_DIM] * mixed)
    p = prod_ref[...]
    y = p * lax.rsqrt(jnp.mean(p * p, axis=-1, keepdims=True) + EPS) * og_ref[...]
    out_ref[...] = y.astype(out_ref.dtype)


def _gmlp(main, ln_gain, w_spatial, b_spatial_t, out_gain):
    T = main.shape[0]
    uv_block = 4 * D_MLSTM // (2 * D_GMLP)
    return pl.pallas_call(
        _gmlp_kernel,
        grid=(T // TG_GMLP,),
        in_specs=[
            pl.BlockSpec((TG_GMLP, 2 * D_GMLP), lambda i: (i, uv_block)),
            pl.BlockSpec((1, D_GMLP), lambda i: (0, 0)),
            pl.BlockSpec((G_GROUPS, CHUNK, CHUNK), lambda i: (0, 0, 0)),
            pl.BlockSpec((CHUNK, G_GROUPS), lambda i: (0, 0)),
            pl.BlockSpec((1, D_GMLP), lambda i: (0, 0)),
        ],
        out_specs=pl.BlockSpec((TG_GMLP, D_GMLP), lambda i: (i, 0)),
        out_shape=jax.ShapeDtypeStruct((T, D_GMLP), BF16),
        scratch_shapes=[pltpu.VMEM((TG_GMLP, D_GMLP), F32)],
        compiler_params=_cparams(),
        name="gmlp",
    )(main, ln_gain, w_spatial, b_spatial_t, out_gain)


def _outproj_kernel(hm_ref, hg_ref, x_ref, wo_ref, g2_ref, wr_ref, br_ref, tril_ref,
                    h_ref, hn_ref, route_ref, cnt_ref, carry_ref):
    i = pl.program_id(0)
    tm = TM_ROUTE

    @pl.when(i == 0)
    def _():
        carry_ref[...] = jnp.zeros_like(carry_ref)

    y = (jnp.dot(hm_ref[...], wo_ref[0:D_MLSTM, :], preferred_element_type=F32)
         + jnp.dot(hg_ref[...], wo_ref[D_MLSTM:D_MODEL, :], preferred_element_type=F32))
    h = x_ref[...] + y
    h_ref[...] = h
    hn = h * lax.rsqrt(jnp.mean(h * h, axis=-1, keepdims=True) + EPS) * g2_ref[...]
    hn_ref[...] = hn

    logits = jnp.dot(hn.astype(BF16), wr_ref[...], preferred_element_type=F32) + br_ref[...]
    lane = lax.broadcasted_iota(jnp.int32, (tm, LANES), 1).astype(F32)
    l = jnp.where(lane < N_EXPERTS, logits, -jnp.inf)
    top_v, top_i = [], []
    for _ in range(TOP_K):
        mx = jnp.max(l, axis=-1, keepdims=True)
        ix = jnp.min(jnp.where(l == mx, lane, float(LANES)), axis=-1, keepdims=True)
        top_v.append(mx)
        top_i.append(ix)
        l = jnp.where(lane == ix, -jnp.inf, l)
    ex = [jnp.exp(v - top_v[0]) for v in top_v]
    inv = 1.0 / (ex[0] + ex[1] + ex[2] + ex[3])

    onehot = [lane == ix for ix in top_i]
    mask = (onehot[0].astype(F32) + onehot[1].astype(F32)
            + onehot[2].astype(F32) + onehot[3].astype(F32))
    incl = jnp.dot(tril_ref[...], mask.astype(BF16), preferred_element_type=F32)
    base = carry_ref[0:1, :] + incl - mask
    route = jnp.zeros((tm, LANES), F32)
    for k in range(TOP_K):
        rank = jnp.sum(jnp.where(onehot[k], base, 0.0), axis=-1, keepdims=True)
        route = jnp.where(lane == k, top_i[k], route)
        route = jnp.where(lane == TOP_K + k, rank, route)
        route = jnp.where(lane == 2 * TOP_K + k, ex[k] * inv, route)
    route_ref[...] = route
    carry_ref[...] = carry_ref[...] + incl[tm - 1:tm, :]
    cnt_ref[...] = carry_ref[...]


def _outproj(h_m, h_g, x2, w_out, gain2, w_router, b_router, tril):
    T = x2.shape[0]
    tm = TM_ROUTE
    return pl.pallas_call(
        _outproj_kernel,
        grid=(T // tm,),
        in_specs=[
            pl.BlockSpec((tm, D_MLSTM), lambda i: (i, 0)),
            pl.BlockSpec((tm, D_GMLP), lambda i: (i, 0)),
            pl.BlockSpec((tm, D_MODEL), lambda i: (i, 0)),
            pl.BlockSpec((D_MODEL, D_MODEL), lambda i: (0, 0)),
            pl.BlockSpec((1, D_MODEL), lambda i: (0, 0)),
            pl.BlockSpec((D_MODEL, LANES), lambda i: (0, 0)),
            pl.BlockSpec((1, LANES), lambda i: (0, 0)),
            pl.BlockSpec((tm, tm), lambda i: (0, 0)),
        ],
        out_specs=[
            pl.BlockSpec((tm, D_MODEL), lambda i: (i, 0)),
            pl.BlockSpec((tm, D_MODEL), lambda i: (i, 0)),
            pl.BlockSpec((tm, LANES), lambda i: (i, 0)),
            pl.BlockSpec((SUBLANES, LANES), lambda i: (0, 0)),
        ],
        out_shape=[
            jax.ShapeDtypeStruct((T, D_MODEL), F32),
            jax.ShapeDtypeStruct((T, D_MODEL), F32),
            jax.ShapeDtypeStruct((T, LANES), F32),
            jax.ShapeDtypeStruct((SUBLANES, LANES), F32),
        ],
        scratch_shapes=[pltpu.VMEM((SUBLANES, LANES), F32)],
        compiler_params=_cparams(),
        name="outproj_router",
    )(h_m, h_g, x2, w_out, gain2, w_router, b_router, tril)


def _dest_kernel(route_ref, pstart_ref, dest_ref):
    r = route_ref[...]
    tm = r.shape[0]
    lane = lax.broadcasted_iota(jnp.int32, (tm, LANES), 1).astype(F32)
    ps = pstart_ref[...]
    out = jnp.zeros((tm, LANES), F32)
    for k in range(TOP_K):
        ix = r[:, k:k + 1]
        rank = r[:, TOP_K + k:TOP_K + k + 1]
        d = rank + jnp.sum(jnp.where(lane == ix, ps, 0.0), axis=-1, keepdims=True)
        out = jnp.where(lane == k, d, out)
    dest_ref[...] = out.astype(jnp.int32)


def _dest(route, pstart):
    T = route.shape[0]
    tm = TM_ROUTE
    return pl.pallas_call(
        _dest_kernel,
        grid=(T // tm,),
        in_specs=[
            pl.BlockSpec((tm, LANES), lambda i: (i, 0)),
            pl.BlockSpec((1, LANES), lambda i: (0, 0)),
        ],
        out_specs=pl.BlockSpec((tm, LANES), lambda i: (i, 0)),
        out_shape=jax.ShapeDtypeStruct((T, LANES), jnp.int32),
        compiler_params=_cparams(),
        name="dest",
    )(route, pstart)


def _dispatch_kernel(dest_ref, pend_ref, hn_ref, xs_hbm, zero_ref, sem, zsem):
    i = pl.program_id(0)
    base = i * TD_DISPATCH

    @pl.when(i == 0)
    def _():
        zero_ref[...] = jnp.zeros_like(zero_ref)

        def zero_copy(e):
            start = pl.multiple_of(pend_ref[e] - BM_FFN, BM_FFN)
            return pltpu.make_async_copy(zero_ref, xs_hbm.at[pl.ds(start, BM_FFN)], zsem)

        def nonempty(e):
            return pend_ref[e] > (pend_ref[e - 1] if e > 0 else 0)

        nb = xs_hbm.shape[0] // BM_FFN

        def tail_copy(j):
            return pltpu.make_async_copy(
                zero_ref, xs_hbm.at[pl.ds((nb - 1 - j) * BM_FFN, BM_FFN)], zsem)

        def unused(j):
            return (nb - 1 - j) * BM_FFN >= pend_ref[N_EXPERTS - 1]

        for e in range(N_EXPERTS):
            @pl.when(nonempty(e))
            def _():
                zero_copy(e).start()

            @pl.when(unused(e))
            def _():
                tail_copy(e).start()
        for e in range(N_EXPERTS):
            @pl.when(nonempty(e))
            def _():
                zero_copy(e).wait()

            @pl.when(unused(e))
            def _():
                tail_copy(e).wait()

    def row_copy(t, k):
        d = dest_ref[(base + t) * TOP_K + k]
        return pltpu.make_async_copy(hn_ref.at[pl.ds(t, 1)], xs_hbm.at[pl.ds(d, 1)], sem)

    def issue(t, carry):
        for k in range(TOP_K):
            row_copy(t, k).start()
        return carry

    def drain(t, carry):
        for k in range(TOP_K):
            row_copy(t, k).wait()
        return carry

    lax.fori_loop(0, TD_DISPATCH, issue, 0, unroll=8)
    lax.fori_loop(0, TD_DISPATCH, drain, 0, unroll=8)


def _dispatch(dest_flat, pend, hn, nbuf):
    T = hn.shape[0]
    return pl.pallas_call(
        _dispatch_kernel,
        grid_spec=pltpu.PrefetchScalarGridSpec(
            num_scalar_prefetch=2,
            grid=(T // TD_DISPATCH,),
            in_specs=[pl.BlockSpec((TD_DISPATCH, D_MODEL), lambda i, d, p: (i, 0))],
            out_specs=pl.BlockSpec(memory_space=pl.ANY),
            scratch_shapes=[
                pltpu.VMEM((BM_FFN, D_MODEL), F32),
                pltpu.SemaphoreType.DMA,
                pltpu.SemaphoreType.DMA,
            ],
        ),
        out_shape=jax.ShapeDtypeStruct((nbuf, D_MODEL), F32),
        compiler_params=_cparams(),
        name="dispatch",
    )(dest_flat, pend, hn)


def _ffn_kernel(be_ref, nv_ref, xs_ref, wup_ref, bup_ref, wdn_ref, bdn_ref, y_ref,
                wup_bf, wdn_bf):
    i = pl.program_id(0)
    e = be_ref[i]
    e_prev = be_ref[jnp.maximum(i - 1, 0)]
    first = jnp.logical_or(i == 0, e != e_prev)
    valid = i < nv_ref[0]

    @pl.when(jnp.logical_and(first, valid))
    def _():
        wup_bf[...] = wup_ref[0].astype(BF16)
        wdn_bf[...] = wdn_ref[0].astype(BF16)

    @pl.when(valid)
    def _():
        xb = xs_ref[...].astype(BF16)
        hb = jnp.dot(xb, wup_bf[...], preferred_element_type=F32) + bup_ref[0]
        glu = jnp.minimum(hb[:, 0:D_FF], SWIGLU_LIMIT)
        lin = jnp.clip(hb[:, D_FF:2 * D_FF], -SWIGLU_LIMIT, SWIGLU_LIMIT)
        act = glu * _sigmoid(SWIGLU_ALPHA * glu) * (lin + 1.0)
        y_ref[...] = (jnp.dot(act.astype(BF16), wdn_bf[...], preferred_element_type=F32)
                      + bdn_ref[0])

    @pl.when(jnp.logical_not(valid))
    def _():
        y_ref[...] = jnp.zeros_like(y_ref)


def _ffn(block_e, n_valid, xs, w_up, b_up3, w_down, b_down3):
    nbuf = xs.shape[0]
    nb = nbuf // BM_FFN
    return pl.pallas_call(
        _ffn_kernel,
        grid_spec=pltpu.PrefetchScalarGridSpec(
            num_scalar_prefetch=2,
            grid=(nb,),
            in_specs=[
                pl.BlockSpec((BM_FFN, D_MODEL), lambda i, be, nv: (jnp.minimum(i, nv[0] - 1), 0)),
                pl.BlockSpec((1, D_MODEL, 2 * D_FF), lambda i, be, nv: (be[i], 0, 0)),
                pl.BlockSpec((1, 1, 2 * D_FF), lambda i, be, nv: (be[i], 0, 0)),
                pl.BlockSpec((1, D_FF, D_MODEL), lambda i, be, nv: (be[i], 0, 0)),
                pl.BlockSpec((1, 1, D_MODEL), lambda i, be, nv: (be[i], 0, 0)),
            ],
            out_specs=pl.BlockSpec((BM_FFN, D_MODEL), lambda i, be, nv: (i, 0)),
            scratch_shapes=[
                pltpu.VMEM((D_MODEL, 2 * D_FF), BF16),
                pltpu.VMEM((D_FF, D_MODEL), BF16),
            ],
        ),
        out_shape=jax.ShapeDtypeStruct((nbuf, D_MODEL), F32),
        compiler_params=_cparams(),
        name="expert_ffn",
    )(block_e, n_valid, xs, w_up, b_up3, w_down, b_down3)


def _combine_kernel(dest_ref, route_ref, h_ref, gain_ref, yb_hbm, out_ref, buf_ref, sem):
    tm = TM_COMBINE
    base = pl.program_id(0) * tm

    def row_copy(t, k):
        d = dest_ref[(base + t) * TOP_K + k]
        return pltpu.make_async_copy(yb_hbm.at[pl.ds(d, 1)], buf_ref.at[k, pl.ds(t, 1)], sem)

    def issue(t, carry):
        for k in range(TOP_K):
            row_copy(t, k).start()
        return carry

    def drain(t, carry):
        for k in range(TOP_K):
            row_copy(t, k).wait()
        return carry

    lax.fori_loop(0, tm, issue, 0, unroll=8)
    lax.fori_loop(0, tm, drain, 0, unroll=8)

    r = route_ref[...]
    acc = h_ref[...]
    for k in range(TOP_K):
        acc = acc + r[:, 2 * TOP_K + k:2 * TOP_K + k + 1] * buf_ref[k]
    out = acc * lax.rsqrt(jnp.mean(acc * acc, axis=-1, keepdims=True) + EPS) * gain_ref[...]
    out_ref[...] = out


def _combine(dest_flat, route, h, gain, yb):
    T = h.shape[0]
    tm = TM_COMBINE
    return pl.pallas_call(
        _combine_kernel,
        grid_spec=pltpu.PrefetchScalarGridSpec(
            num_scalar_prefetch=1,
            grid=(T // tm,),
            in_specs=[
                pl.BlockSpec((tm, LANES), lambda i, d: (i, 0)),
                pl.BlockSpec((tm, D_MODEL), lambda i, d: (i, 0)),
                pl.BlockSpec((1, D_MODEL), lambda i, d: (0, 0)),
                pl.BlockSpec(memory_space=pl.ANY),
            ],
            out_specs=pl.BlockSpec((tm, D_MODEL), lambda i, d: (i, 0)),
            scratch_shapes=[
                pltpu.VMEM((TOP_K, tm, D_MODEL), F32),
                pltpu.SemaphoreType.DMA,
            ],
        ),
        out_shape=jax.ShapeDtypeStruct((T, D_MODEL), F32),
        compiler_params=_cparams(),
        name="combine",
    )(dest_flat, route, h, gain, yb)


def _layer(x2, B, S, norm1_gain, w_in, conv_qk, b_igate, b_fgate, mlstm_norm_gain, gmlp_ln_gain,
           w_spatial, b_spatial, gmlp_out_gain, w_out, norm2_gain, w_router, b_router,
           w_up, b_up, w_down, b_down):
    T = B * S
    n_gate = 2 * M_HEADS
    g0 = 4 * D_MLSTM
    w_main = jnp.concatenate([w_in[:, :g0], w_in[:, g0 + n_gate:]], axis=1).astype(BF16)
    w_gate = jnp.pad(w_in[:, g0:g0 + n_gate], ((0, 0), (0, LANES - n_gate))).astype(BF16)
    gbias = jnp.pad(jnp.concatenate([b_igate, b_fgate]), (0, LANES - n_gate)).reshape(1, LANES)

    main, gates = _inproj(x2, norm1_gain.reshape(1, D_MODEL), w_main, w_gate)
    h_m = _mlstm(main.reshape(B, S, D_MAIN), gates.reshape(B, S, LANES), conv_qk, gbias,
                 mlstm_norm_gain.reshape(1, D_MLSTM)).reshape(T, D_MLSTM)
    h_g = _gmlp(main, gmlp_ln_gain.reshape(1, D_GMLP), w_spatial, b_spatial.T,
                gmlp_out_gain.reshape(1, D_GMLP))

    w_r = jnp.pad(w_router, ((0, 0), (0, LANES - N_EXPERTS))).astype(BF16)
    b_r = jnp.pad(b_router, (0, LANES - N_EXPERTS)).reshape(1, LANES)
    tril = jnp.tril(jnp.ones((TM_ROUTE, TM_ROUTE), BF16))
    h, hn, route, cnt = _outproj(h_m, h_g, x2, w_out.astype(BF16),
                                 norm2_gain.reshape(1, D_MODEL), w_r, b_r, tril)

    counts = cnt[0, :N_EXPERTS].astype(jnp.int32)
    padded = (counts + BM_FFN - 1) // BM_FFN * BM_FFN
    pend = jnp.cumsum(padded)
    pstart = pend - padded
    nb = (T * TOP_K) // BM_FFN + N_EXPERTS
    n_valid = (pend[-1] // BM_FFN).astype(jnp.int32)
    blk = jnp.arange(nb, dtype=jnp.int32)
    first_row = jnp.minimum(blk, n_valid - 1) * BM_FFN
    block_e = jnp.sum((pend[None, :] <= first_row[:, None]).astype(jnp.int32), axis=1)
    block_e = jnp.minimum(block_e, N_EXPERTS - 1)
    pstart_row = jnp.pad(pstart.astype(F32), (0, LANES - N_EXPERTS)).reshape(1, LANES)

    dest = _dest(route, pstart_row)
    dest_flat = dest[:, :TOP_K].reshape(T * TOP_K)

    xs = _dispatch(dest_flat, pend.astype(jnp.int32), hn, nb * BM_FFN)
    yb = _ffn(block_e, n_valid.reshape(1), xs, w_up, b_up.reshape(N_EXPERTS, 1, 2 * D_FF),
              w_down, b_down.reshape(N_EXPERTS, 1, D_MODEL))
    return dest_flat, route, h, yb


def kernel(x, norm1_gain, w_in, conv_qk, b_igate, b_fgate, mlstm_norm_gain, gmlp_ln_gain,
           w_spatial, b_spatial, gmlp_out_gain, w_out, norm2_gain, w_router, b_router,
           w_up, b_up, w_down, b_down, final_gain):
    B, S, D = x.shape
    depth = norm1_gain.shape[0]
    assert depth == 1 and D == D_MODEL and S % CHUNK == 0
    x2 = x.reshape(B * S, D)
    l = 0
    dest_flat, route, h, yb = _layer(
        x2, B, S, norm1_gain[l], w_in[l], conv_qk[l], b_igate[l], b_fgate[l], mlstm_norm_gain[l],
        gmlp_ln_gain[l], w_spatial[l], b_spatial[l], gmlp_out_gain[l], w_out[l], norm2_gain[l],
        w_router[l], b_router[l], w_up[l], b_up[l], w_down[l], b_down[l])
    out = _combine(dest_flat, route, h, final_gain.reshape(1, D_MODEL), yb)
    return out.reshape(B, S, D)
```

```python
import functools

import jax
import jax.numpy as jnp
from jax import lax
from jax.experimental import pallas as pl
from jax.experimental.pallas import tpu as pltpu

F32 = jnp.float32
BF16 = jnp.bfloat16

D_MODEL = 1024
M_HEADS = 4
M_HEAD_DIM = 128
D_MLSTM = M_HEADS * M_HEAD_DIM
D_GMLP = D_MODEL - D_MLSTM
G_GROUPS = 4
G_DIM = D_GMLP // G_GROUPS
CHUNK = 128
CONV_K = 4
N_EXPERTS = 32
TOP_K = 4
D_FF = D_MODEL
SWIGLU_LIMIT = 7.0
SWIGLU_ALPHA = 1.702
EPS = 1e-6

LANES = 128
SUBLANES = 8
D_MAIN = 4 * D_MLSTM + 2 * D_GMLP
TM_PROJ = 512
TG_GMLP = 512
TM_ROUTE = 512
TM_COMBINE = 512
BM_FFN = 256
VMEM_LIMIT = 56 * 1024 * 1024


def _cparams(n_axes=1):
    return pltpu.CompilerParams(
        dimension_semantics=("arbitrary",) * n_axes, vmem_limit_bytes=VMEM_LIMIT)


def _sigmoid(x):
    return 1.0 / (1.0 + jnp.exp(-x))


def _gelu_tanh(x):
    c = 0.7978845608028654
    return x * (0.5 * (1.0 + jnp.tanh(c * (x + 0.044715 * (x * x * x)))))


def _log_sigmoid(x):
    return jnp.minimum(x, 0.0) - jnp.log1p(jnp.exp(-jnp.abs(x)))


def _inproj_kernel(x_ref, gain_ref, wm_ref, wg_ref, main_ref, gate_ref):
    x = x_ref[...]
    xn = x * lax.rsqrt(jnp.mean(x * x, axis=-1, keepdims=True) + EPS) * gain_ref[...]
    xb = xn.astype(BF16)
    main_ref[...] = jnp.dot(xb, wm_ref[...], preferred_element_type=F32).astype(BF16)
    gate_ref[...] = jnp.dot(xb, wg_ref[...], preferred_element_type=F32)


def _inproj(x2, gain, w_main, w_gate):
    T = x2.shape[0]
    return pl.pallas_call(
        _inproj_kernel,
        grid=(T // TM_PROJ,),
        in_specs=[
            pl.BlockSpec((TM_PROJ, D_MODEL), lambda i: (i, 0)),
            pl.BlockSpec((1, D_MODEL), lambda i: (0, 0)),
            pl.BlockSpec((D_MODEL, D_MAIN), lambda i: (0, 0)),
            pl.BlockSpec((D_MODEL, LANES), lambda i: (0, 0)),
        ],
        out_specs=[
            pl.BlockSpec((TM_PROJ, D_MAIN), lambda i: (i, 0)),
            pl.BlockSpec((TM_PROJ, LANES), lambda i: (i, 0)),
        ],
        out_shape=[
            jax.ShapeDtypeStruct((T, D_MAIN), BF16),
            jax.ShapeDtypeStruct((T, LANES), F32),
        ],
        compiler_params=_cparams(),
        name="inproj",
    )(x2, gain, w_main, w_gate)


def _mlstm_kernel(qkvo_ref, gates_ref, conv_ref, gbias_ref, ngain_ref, out_ref,
                  xp_ref, ct_ref, n_ref, m_ref, *, batch):
    c = pl.program_id(0)
    L, DH = CHUNK, M_HEAD_DIM

    @pl.when(c == 0)
    def _():
        xp_ref[:, 0:SUBLANES, :] = jnp.zeros((batch, SUBLANES, 2 * D_MLSTM), F32)
        ct_ref[...] = jnp.zeros_like(ct_ref)
        n_ref[...] = jnp.zeros_like(n_ref)
        m_ref[...] = jnp.zeros_like(m_ref)

    row = lax.broadcasted_iota(jnp.int32, (L, L), 0)
    col = lax.broadcasted_iota(jnp.int32, (L, L), 1)
    tril = row >= col
    tril_f = tril.astype(F32)

    for b in range(batch):
        xp_ref[b, SUBLANES:SUBLANES + L, :] = qkvo_ref[b, :, 0:2 * D_MLSTM].astype(F32)
        off = SUBLANES - (CONV_K - 1)
        acc = xp_ref[b, off:off + L, :] * conv_ref[0:1, :]
        for j in range(1, CONV_K):
            acc = acc + xp_ref[b, off + j:off + j + L, :] * conv_ref[j:j + 1, :]
        xp_ref[b, 0:SUBLANES, :] = xp_ref[b, L:L + SUBLANES, :]
        qk = acc * _sigmoid(acc)

        gc = gates_ref[b] + gbias_ref[...]
        lf = _log_sigmoid(gc)
        ball = jnp.dot(tril_f, lf, precision=lax.Precision.HIGHEST,
                       preferred_element_type=F32)
        gc_t = gc.T
        ball_t = ball.T

        for h in range(M_HEADS):
            s = b * M_HEADS + h
            q = qk[:, h * DH:(h + 1) * DH]
            k = qk[:, D_MLSTM + h * DH:D_MLSTM + (h + 1) * DH] * (DH ** -0.5)
            v = qkvo_ref[b, :, 2 * D_MLSTM + h * DH:2 * D_MLSTM + (h + 1) * DH]
            o = qkvo_ref[b, :, 3 * D_MLSTM + h * DH:3 * D_MLSTM + (h + 1) * DH].astype(F32)
            qb = q.astype(BF16)
            kb = k.astype(BF16)

            b_col = ball[:, M_HEADS + h:M_HEADS + h + 1]
            b_row = ball_t[M_HEADS + h:M_HEADS + h + 1, :]
            li_col = gc[:, h:h + 1]
            li_row = gc_t[h:h + 1, :]
            b_last = b_col[L - 1:L, :]
            m_prev = m_ref[s][0:1, 0:1]
            ct_prev = ct_ref[s]
            n_prev = n_ref[s]

            dmat = jnp.where(tril, b_col - b_row + li_row, -jnp.inf)
            a_inter = b_col + m_prev
            m_t = jnp.maximum(a_inter, jnp.max(dmat, axis=-1, keepdims=True))
            qkt = lax.dot_general(qb, kb, (((1,), (1,)), ((), ())), preferred_element_type=F32)
            w = jnp.exp(dmat - m_t) * qkt
            inter = jnp.exp(a_inter - m_t)
            num = (jnp.dot(w.astype(BF16), v, preferred_element_type=F32)
                   + inter * jnp.dot(qb, ct_prev.astype(BF16), preferred_element_type=F32))
            den = (jnp.sum(w, axis=-1, keepdims=True)
                   + inter * jnp.sum(q * n_prev, axis=-1, keepdims=True))
            hh = num * (1.0 / jnp.maximum(jnp.abs(den), jnp.exp(-m_t)))

            mu = jnp.mean(hh, axis=-1, keepdims=True)
            xc = hh - mu
            y = xc * lax.rsqrt(jnp.mean(xc * xc, axis=-1, keepdims=True) + EPS)
            y = y * ngain_ref[:, h * DH:(h + 1) * DH]
            out_ref[b, :, h * DH:(h + 1) * DH] = (_sigmoid(o) * y).astype(out_ref.dtype)

            g_col = b_last - b_col + li_col
            m_loc = jnp.max(g_col, axis=0, keepdims=True)
            w_col = jnp.exp(g_col - m_loc)
            wv = (w_col * v.astype(F32)).astype(BF16)
            ct_loc = jnp.dot(k.T.astype(BF16), wv, preferred_element_type=F32)
            n_loc = jnp.sum(w_col * k, axis=0, keepdims=True)
            m_new = jnp.maximum(b_last + m_prev, m_loc)
            a = jnp.exp(b_last + m_prev - m_new)
            cc = jnp.exp(m_loc - m_new)
            ct_ref[s] = a * ct_prev + cc * ct_loc
            n_ref[s] = a * n_prev + cc * n_loc
            m_ref[s] = jnp.broadcast_to(m_new, (SUBLANES, LANES))


def _mlstm(main3, gates3, conv_qk, gbias, ngain):
    B, S, _ = main3.shape
    nc = S // CHUNK
    return pl.pallas_call(
        functools.partial(_mlstm_kernel, batch=B),
        grid=(nc,),
        in_specs=[
            pl.BlockSpec((B, CHUNK, 4 * D_MLSTM), lambda c: (0, c, 0)),
            pl.BlockSpec((B, CHUNK, LANES), lambda c: (0, c, 0)),
            pl.BlockSpec((CONV_K, 2 * D_MLSTM), lambda c: (0, 0)),
            pl.BlockSpec((1, LANES), lambda c: (0, 0)),
            pl.BlockSpec((1, D_MLSTM), lambda c: (0, 0)),
        ],
        out_specs=pl.BlockSpec((B, CHUNK, D_MLSTM), lambda c: (0, c, 0)),
        out_shape=jax.ShapeDtypeStruct((B, S, D_MLSTM), BF16),
        scratch_shapes=[
            pltpu.VMEM((B, CHUNK + SUBLANES, 2 * D_MLSTM), F32),
            pltpu.VMEM((B * M_HEADS, M_HEAD_DIM, M_HEAD_DIM), F32),
            pltpu.VMEM((B * M_HEADS, 1, M_HEAD_DIM), F32),
            pltpu.VMEM((B * M_HEADS, SUBLANES, LANES), F32),
        ],
        compiler_params=_cparams(),
        name="mlstm",
    )(main3, gates3, conv_qk, gbias, ngain)


def _gmlp_kernel(uv_ref, lng_ref, ws_ref, bs_ref, og_ref, out_ref, prod_ref):
    L = CHUNK
    u = _gelu_tanh(uv_ref[:, 0:D_GMLP].astype(F32))
    vg = _gelu_tanh(uv_ref[:, D_GMLP:2 * D_GMLP].astype(F32))
    mu = jnp.mean(vg, axis=-1, keepdims=True)
    xc = vg - mu
    vg = xc * lax.rsqrt(jnp.mean(xc * xc, axis=-1, keepdims=True) + EPS) * lng_ref[...]
    vb = vg.astype(BF16)
    row = lax.broadcasted_iota(jnp.int32, (L, L), 0)
    col = lax.broadcasted_iota(jnp.int32, (L, L), 1)
    tril = row >= col
    for g in range(G_GROUPS):
        wg = jnp.where(tril, ws_ref[g], 0.0).astype(BF16)
        b_col = bs_ref[:, g:g + 1]
        for j in range(TG_GMLP // L):
            mixed = jnp.dot(wg, vb[j * L:(j + 1) * L, g * G_DIM:(g + 1) * G_DIM],
                            preferred_element_type=F32) + b_col
            prod_ref[j * L:(j + 1) * L, g * G_DIM:(g + 1) * G_DIM] = (
                u[j * L:(j + 1) * L, g * G_DIM:(g + 1) * G_DIM] * mixed)
    p = prod_ref[...]
    y = p * lax.rsqrt(jnp.mean(p * p, axis=-1, keepdims=True) + EPS) * og_ref[...]
    out_ref[...] = y.astype(out_ref.dtype)


def _gmlp(main, ln_gain, w_spatial, b_spatial_t, out_gain):
    T = main.shape[0]
    uv_block = 4 * D_MLSTM // (2 * D_GMLP)
    return pl.pallas_call(
        _gmlp_kernel,
        grid=(T // TG_GMLP,),
        in_specs=[
            pl.BlockSpec((TG_GMLP, 2 * D_GMLP), lambda i: (i, uv_block)),
            pl.BlockSpec((1, D_GMLP), lambda i: (0, 0)),
            pl.BlockSpec((G_GROUPS, CHUNK, CHUNK), lambda i: (0, 0, 0)),
            pl.BlockSpec((CHUNK, G_GROUPS), lambda i: (0, 0)),
            pl.BlockSpec((1, D_GMLP), lambda i: (0, 0)),
        ],
        out_specs=pl.BlockSpec((TG_GMLP, D_GMLP), lambda i: (i, 0)),
        out_shape=jax.ShapeDtypeStruct((T, D_GMLP), BF16),
        scratch_shapes=[pltpu.VMEM((TG_GMLP, D_GMLP), F32)],
        compiler_params=_cparams(),
        name="gmlp",
    )(main, ln_gain, w_spatial, b_spatial_t, out_gain)


def _outproj_kernel(hm_ref, hg_ref, x_ref, wo_ref, g2_ref, wr_ref, br_ref, tril_ref,
                    h_ref, hn_ref, route_ref, cnt_ref, carry_ref):
    i = pl.program_id(0)
    tm = TM_ROUTE

    @pl.when(i == 0)
    def _():
        carry_ref[...] = jnp.zeros_like(carry_ref)

    y = (jnp.dot(hm_ref[...], wo_ref[0:D_MLSTM, :], preferred_element_type=F32)
         + jnp.dot(hg_ref[...], wo_ref[D_MLSTM:D_MODEL, :], preferred_element_type=F32))
    h = x_ref[...] + y
    h_ref[...] = h
    hn = h * lax.rsqrt(jnp.mean(h * h, axis=-1, keepdims=True) + EPS) * g2_ref[...]
    hn_ref[...] = hn

    logits = jnp.dot(hn.astype(BF16), wr_ref[...], preferred_element_type=F32) + br_ref[...]
    lane = lax.broadcasted_iota(jnp.int32, (tm, LANES), 1).astype(F32)
    l = jnp.where(lane < N_EXPERTS, logits, -jnp.inf)
    top_v, top_i = [], []
    for _ in range(TOP_K):
        mx = jnp.max(l, axis=-1, keepdims=True)
        ix = jnp.min(jnp.where(l == mx, lane, float(LANES)), axis=-1, keepdims=True)
        top_v.append(mx)
        top_i.append(ix)
        l = jnp.where(lane == ix, -jnp.inf, l)
    ex = [jnp.exp(v - top_v[0]) for v in top_v]
    inv = 1.0 / (ex[0] + ex[1] + ex[2] + ex[3])

    onehot = [lane == ix for ix in top_i]
    mask = (onehot[0].astype(F32) + onehot[1].astype(F32)
            + onehot[2].astype(F32) + onehot[3].astype(F32))
    incl = jnp.dot(tril_ref[...], mask.astype(BF16), preferred_element_type=F32)
    base = carry_ref[0:1, :] + incl - mask
    route = jnp.zeros((tm, LANES), F32)
    for k in range(TOP_K):
        rank = jnp.sum(jnp.where(onehot[k], base, 0.0), axis=-1, keepdims=True)
        route = jnp.where(lane == k, top_i[k], route)
        route = jnp.where(lane == TOP_K + k, rank, route)
        route = jnp.where(lane == 2 * TOP_K + k, ex[k] * inv, route)
    route_ref[...] = route
    carry_ref[...] = carry_ref[...] + incl[tm - 1:tm, :]
    cnt_ref[...] = carry_ref[...]


def _outproj(h_m, h_g, x2, w_out, gain2, w_router, b_router, tril):
    T = x2.shape[0]
    tm = TM_ROUTE
    return pl.pallas_call(
        _outproj_kernel,
        grid=(T // tm,),
        in_specs=[
            pl.BlockSpec((tm, D_MLSTM), lambda i: (i, 0)),
            pl.BlockSpec((tm, D_GMLP), lambda i: (i, 0)),
            pl.BlockSpec((tm, D_MODEL), lambda i: (i, 0)),
            pl.BlockSpec((D_MODEL, D_MODEL), lambda i: (0, 0)),
            pl.BlockSpec((1, D_MODEL), lambda i: (0, 0)),
            pl.BlockSpec((D_MODEL, LANES), lambda i: (0, 0)),
            pl.BlockSpec((1, LANES), lambda i: (0, 0)),
            pl.BlockSpec((tm, tm), lambda i: (0, 0)),
        ],
        out_specs=[
            pl.BlockSpec((tm, D_MODEL), lambda i: (i, 0)),
            pl.BlockSpec((tm, D_MODEL), lambda i: (i, 0)),
            pl.BlockSpec((tm, LANES), lambda i: (i, 0)),
            pl.BlockSpec((SUBLANES, LANES), lambda i: (0, 0)),
        ],
        out_shape=[
            jax.ShapeDtypeStruct((T, D_MODEL), F32),
            jax.ShapeDtypeStruct((T, D_MODEL), F32),
            jax.ShapeDtypeStruct((T, LANES), F32),
            jax.ShapeDtypeStruct((SUBLANES, LANES), F32),
        ],
        scratch_shapes=[pltpu.VMEM((SUBLANES, LANES), F32)],
        compiler_params=_cparams(),
        name="outproj_router",
    )(h_m, h_g, x2, w_out, gain2, w_router, b_router, tril)


def _dest_kernel(route_ref, pstart_ref, dest_ref):
    r = route_ref[...]
    tm = r.shape[0]
    lane = lax.broadcasted_iota(jnp.int32, (tm, LANES), 1).astype(F32)
    ps = pstart_ref[...]
    out = jnp.zeros((tm, LANES), F32)
    for k in range(TOP_K):
        ix = r[:, k:k + 1]
        rank = r[:, TOP_K + k:TOP_K + k + 1]
        d = rank + jnp.sum(jnp.where(lane == ix, ps, 0.0), axis=-1, keepdims=True)
        out = jnp.where(lane == k, d, out)
    dest_ref[...] = out.astype(jnp.int32)


def _dest(route, pstart):
    T = route.shape[0]
    tm = TM_ROUTE
    return pl.pallas_call(
        _dest_kernel,
        grid=(T // tm,),
        in_specs=[
            pl.BlockSpec((tm, LANES), lambda i: (i, 0)),
            pl.BlockSpec((1, LANES), lambda i: (0, 0)),
        ],
        out_specs=pl.BlockSpec((tm, LANES), lambda i: (i, 0)),
        out_shape=jax.ShapeDtypeStruct((T, LANES), jnp.int32),
        compiler_params=_cparams(),
        name="dest",
    )(route, pstart)


INV_LEAD = 2 * BM_FFN


INV_STEPS = 32


def _inv_kernel(dest_ref, seg_ref, inv_ref, *, n_tok):
    n_assign = n_tok * TOP_K
    n_init = inv_ref.shape[0] // INV_STEPS
    n_fill = n_assign // INV_STEPS
    step = pl.program_id(0)

    @pl.when(step < INV_STEPS)
    def _():
        base = step * n_init

        def init(q, carry):
            inv_ref[base + q] = n_assign + jnp.minimum(base + q, INV_LEAD - 1)
            return carry

        lax.fori_loop(0, n_init, init, 0, unroll=8)

    @pl.when(jnp.logical_and(step >= INV_STEPS, step < INV_STEPS + N_EXPERTS))
    def _():
        e = step - INV_STEPS
        first_pad = seg_ref[e]
        n_pad = seg_ref[N_EXPERTS + e] - first_pad

        def pad(j, carry):
            inv_ref[INV_LEAD + first_pad + j] = n_assign + INV_LEAD + e * BM_FFN + j
            return carry

        lax.fori_loop(0, n_pad, pad, 0)

    @pl.when(step >= INV_STEPS + N_EXPERTS)
    def _():
        base = (step - INV_STEPS - N_EXPERTS) * n_fill

        def fill(j, carry):
            a = base + j
            inv_ref[dest_ref[a] + INV_LEAD] = (a % TOP_K) * n_tok + a // TOP_K
            return carry

        lax.fori_loop(0, n_fill, fill, 0, unroll=8)


def _inv(dest_flat, seg, n_inv, n_tok):
    assert n_inv % INV_STEPS == 0 and (n_tok * TOP_K) % INV_STEPS == 0
    return pl.pallas_call(
        functools.partial(_inv_kernel, n_tok=n_tok),
        grid=(2 * INV_STEPS + N_EXPERTS,),
        in_specs=[pl.BlockSpec(memory_space=pltpu.SMEM), pl.BlockSpec(memory_space=pltpu.SMEM)],
        out_specs=pl.BlockSpec(memory_space=pltpu.SMEM),
        out_shape=jax.ShapeDtypeStruct((n_inv,), jnp.int32),
        compiler_params=_cparams(),
        name="inverse_map",
    )(dest_flat, seg)


def _ffn_kernel(be_ref, nv_ref, inv_ref, hn_hbm, wup_ref, bup_ref, wdn_ref, bdn_ref, yt_hbm,
                xs0, xs1, y0, y1, wup_bf, wdn_bf, gsem, ssem, *, n_tok):
    bm = BM_FFN
    i = pl.program_id(0)
    nv = nv_ref[0]
    e = be_ref[i]
    e_prev = be_ref[jnp.maximum(i - 1, 0)]
    first = jnp.logical_or(i == 0, e != e_prev)
    valid = i < nv
    xs = (xs0, xs1)
    ys = (y0, y1)
    assert n_tok & (n_tok - 1) == 0

    def gather_row(block, s, r):
        tok = inv_ref[block * bm + (INV_LEAD + r)] & (n_tok - 1)
        return pltpu.make_async_copy(hn_hbm.at[pl.ds(tok, 1)], xs[s].at[pl.ds(r, 1)], gsem.at[s])

    def scatter_row(block, s, r):
        row = inv_ref[block * bm + (INV_LEAD + r)]
        return pltpu.make_async_copy(ys[s].at[pl.ds(r, 1)], yt_hbm.at[pl.ds(row, 1)], ssem.at[s])

    def gather_wait(s):
        for r in range(bm):
            pltpu.make_async_copy(hn_hbm.at[pl.ds(0, 1)], xs[s].at[pl.ds(r, 1)], gsem.at[s]).wait()

    def scatter_wait(s):
        for r in range(bm):
            pltpu.make_async_copy(ys[s].at[pl.ds(r, 1)], yt_hbm.at[pl.ds(0, 1)], ssem.at[s]).wait()

    @pl.when(i == 0)
    def _():
        for r in range(bm):
            gather_row(0, 0, r).start()
        y1[...] = jnp.zeros_like(y1)
        n_spare = (yt_hbm.shape[0] - n_tok * TOP_K) // bm
        spare_fill = [
            pltpu.make_async_copy(y1, yt_hbm.at[pl.ds(n_tok * TOP_K + j * bm, bm)], ssem.at[0])
            for j in range(n_spare)]
        for cp in spare_fill:
            cp.start()
        for cp in spare_fill:
            cp.wait()
        gather_wait(0)

    @pl.when(jnp.logical_and(first, valid))
    def _():
        wup_bf[...] = wup_ref[0].astype(BF16)
        wdn_bf[...] = wdn_ref[0].astype(BF16)

    for s in (0, 1):
        @pl.when(jnp.logical_and(valid, i % 2 == s))
        def _():
            nxt = jnp.minimum(i + 1, nv - 1)
            for r in range(bm):
                gather_row(nxt, 1 - s, r).start()
                scatter_row(i - 1, 1 - s, r).start()
            xb = xs[s][...].astype(BF16)
            hb = jnp.dot(xb, wup_bf[...], preferred_element_type=F32) + bup_ref[0]
            glu = jnp.minimum(hb[:, 0:D_FF], SWIGLU_LIMIT)
            lin = jnp.clip(hb[:, D_FF:2 * D_FF], -SWIGLU_LIMIT, SWIGLU_LIMIT)
            act = glu * _sigmoid(SWIGLU_ALPHA * glu) * (lin + 1.0)
            ys[s][...] = (jnp.dot(act.astype(BF16), wdn_bf[...], preferred_element_type=F32)
                          + bdn_ref[0])
            gather_wait(1 - s)
            scatter_wait(1 - s)

        @pl.when(jnp.logical_and(i == nv - 1, i % 2 == s))
        def _():
            for r in range(bm):
                scatter_row(i, s, r).start()
            scatter_wait(s)


def _ffn(block_e, n_valid, inv, hn, w_up, b_up3, w_down, b_down3):
    n_tok = hn.shape[0]
    nb = (inv.shape[0] - INV_LEAD) // BM_FFN
    n_rows = n_tok * TOP_K + INV_LEAD + N_EXPERTS * BM_FFN
    wspec = lambda shape: pl.BlockSpec(shape, lambda i, be, nv, iv: (be[i], 0, 0))
    return pl.pallas_call(
        functools.partial(_ffn_kernel, n_tok=n_tok),
        grid_spec=pltpu.PrefetchScalarGridSpec(
            num_scalar_prefetch=3,
            grid=(nb,),
            in_specs=[
                pl.BlockSpec(memory_space=pl.ANY),
                wspec((1, D_MODEL, 2 * D_FF)),
                wspec((1, 1, 2 * D_FF)),
                wspec((1, D_FF, D_MODEL)),
                wspec((1, 1, D_MODEL)),
            ],
            out_specs=pl.BlockSpec(memory_space=pl.ANY),
            scratch_shapes=[
                pltpu.VMEM((BM_FFN, D_MODEL), F32),
                pltpu.VMEM((BM_FFN, D_MODEL), F32),
                pltpu.VMEM((BM_FFN, D_MODEL), F32),
                pltpu.VMEM((BM_FFN, D_MODEL), F32),
                pltpu.VMEM((D_MODEL, 2 * D_FF), BF16),
                pltpu.VMEM((D_FF, D_MODEL), BF16),
                pltpu.SemaphoreType.DMA((2,)),
                pltpu.SemaphoreType.DMA((2,)),
            ],
        ),
        out_shape=jax.ShapeDtypeStruct((n_rows, D_MODEL), F32),
        compiler_params=_cparams(),
        name="expert_ffn",
    )(block_e, n_valid, inv, hn, w_up, b_up3, w_down, b_down3)


def _combine_kernel(route_ref, h_ref, gain_ref, y0_ref, y1_ref, y2_ref, y3_ref, out_ref):
    r = route_ref[...]
    acc = h_ref[...]
    for k, y_ref in enumerate((y0_ref, y1_ref, y2_ref, y3_ref)):
        acc = acc + r[:, 2 * TOP_K + k:2 * TOP_K + k + 1] * y_ref[...]
    out = acc * lax.rsqrt(jnp.mean(acc * acc, axis=-1, keepdims=True) + EPS) * gain_ref[...]
    out_ref[...] = out


def _combine(route, h, gain, yt):
    T = h.shape[0]
    tm = TM_COMBINE
    nt = T // tm
    y_specs = [pl.BlockSpec((tm, D_MODEL), functools.partial(lambda i, k: (k * nt + i, 0), k=k))
               for k in range(TOP_K)]
    return pl.pallas_call(
        _combine_kernel,
        grid=(nt,),
        in_specs=[
            pl.BlockSpec((tm, LANES), lambda i: (i, 0)),
            pl.BlockSpec((tm, D_MODEL), lambda i: (i, 0)),
            pl.BlockSpec((1, D_MODEL), lambda i: (0, 0)),
        ] + y_specs,
        out_specs=pl.BlockSpec((tm, D_MODEL), lambda i: (i, 0)),
        out_shape=jax.ShapeDtypeStruct((T, D_MODEL), F32),
        compiler_params=_cparams(),
        name="combine",
    )(route, h, gain, yt, yt, yt, yt)


def _layer(x2, B, S, norm1_gain, w_in, conv_qk, b_igate, b_fgate, mlstm_norm_gain, gmlp_ln_gain,
           w_spatial, b_spatial, gmlp_out_gain, w_out, norm2_gain, w_router, b_router,
           w_up, b_up, w_down, b_down):
    T = B * S
    n_gate = 2 * M_HEADS
    g0 = 4 * D_MLSTM
    w_main = jnp.concatenate([w_in[:, :g0], w_in[:, g0 + n_gate:]], axis=1).astype(BF16)
    w_gate = jnp.pad(w_in[:, g0:g0 + n_gate], ((0, 0), (0, LANES - n_gate))).astype(BF16)
    gbias = jnp.pad(jnp.concatenate([b_igate, b_fgate]), (0, LANES - n_gate)).reshape(1, LANES)

    main, gates = _inproj(x2, norm1_gain.reshape(1, D_MODEL), w_main, w_gate)
    h_m = _mlstm(main.reshape(B, S, D_MAIN), gates.reshape(B, S, LANES), conv_qk, gbias,
                 mlstm_norm_gain.reshape(1, D_MLSTM)).reshape(T, D_MLSTM)
    h_g = _gmlp(main, gmlp_ln_gain.reshape(1, D_GMLP), w_spatial, b_spatial.T,
                gmlp_out_gain.reshape(1, D_GMLP))

    w_r = jnp.pad(w_router, ((0, 0), (0, LANES - N_EXPERTS))).astype(BF16)
    b_r = jnp.pad(b_router, (0, LANES - N_EXPERTS)).reshape(1, LANES)
    tril = jnp.tril(jnp.ones((TM_ROUTE, TM_ROUTE), BF16))
    h, hn, route, cnt = _outproj(h_m, h_g, x2, w_out.astype(BF16),
                                 norm2_gain.reshape(1, D_MODEL), w_r, b_r, tril)

    counts = cnt[0, :N_EXPERTS].astype(jnp.int32)
    padded = (counts + BM_FFN - 1) // BM_FFN * BM_FFN
    pend = jnp.cumsum(padded)
    pstart = pend - padded
    nb = (T * TOP_K) // BM_FFN + N_EXPERTS
    n_valid = (pend[-1] // BM_FFN).astype(jnp.int32)
    blk = jnp.arange(nb, dtype=jnp.int32)
    first_row = jnp.minimum(blk, n_valid - 1) * BM_FFN
    block_e = jnp.sum((pend[None, :] <= first_row[:, None]).astype(jnp.int32), axis=1)
    block_e = jnp.minimum(block_e, N_EXPERTS - 1)
    pstart_row = jnp.pad(pstart.astype(F32), (0, LANES - N_EXPERTS)).reshape(1, LANES)

    dest = _dest(route, pstart_row)
    dest_flat = dest[:, :TOP_K].reshape(T * TOP_K)

    seg = jnp.concatenate([pstart + counts, pend]).astype(jnp.int32)
    inv = _inv(dest_flat, seg, INV_LEAD + nb * BM_FFN, T)
    yt = _ffn(block_e, n_valid.reshape(1), inv, hn, w_up, b_up.reshape(N_EXPERTS, 1, 2 * D_FF),
              w_down, b_down.reshape(N_EXPERTS, 1, D_MODEL))
    return route, h, yt


def kernel(x, norm1_gain, w_in, conv_qk, b_igate, b_fgate, mlstm_norm_gain, gmlp_ln_gain,
           w_spatial, b_spatial, gmlp_out_gain, w_out, norm2_gain, w_router, b_router,
           w_up, b_up, w_down, b_down, final_gain):
    B, S, D = x.shape
    depth = norm1_gain.shape[0]
    assert depth == 1 and D == D_MODEL and S % CHUNK == 0
    x2 = x.reshape(B * S, D)
    l = 0
    route, h, yt = _layer(
        x2, B, S, norm1_gain[l], w_in[l], conv_qk[l], b_igate[l], b_fgate[l], mlstm_norm_gain[l],
        gmlp_ln_gain[l], w_spatial[l], b_spatial[l], gmlp_out_gain[l], w_out[l], norm2_gain[l],
        w_router[l], b_router[l], w_up[l], b_up[l], w_down[l], b_down[l])
    out = _combine(route, h, final_gain.reshape(1, D_MODEL), yt)
    return out.reshape(B, S, D)
```

```python
import functools

import jax
import jax.numpy as jnp
from jax import lax
from jax.experimental import pallas as pl
from jax.experimental.pallas import tpu as pltpu

F32 = jnp.float32
BF16 = jnp.bfloat16

D_MODEL = 1024
M_HEADS = 4
M_HEAD_DIM = 128
D_MLSTM = M_HEADS * M_HEAD_DIM
D_GMLP = D_MODEL - D_MLSTM
G_GROUPS = 4
G_DIM = D_GMLP // G_GROUPS
CHUNK = 128
CONV_K = 4
N_EXPERTS = 32
TOP_K = 4
D_FF = D_MODEL
SWIGLU_LIMIT = 7.0
SWIGLU_ALPHA = 1.702
EPS = 1e-6

LANES = 128
SUBLANES = 8
ROW_TILES = D_MODEL // LANES
assert ROW_TILES == SUBLANES
D_MAIN = 4 * D_MLSTM + 2 * D_GMLP
TM_PROJ = 512
TG_GMLP = 512
TM_ROUTE = 512
TD_DISPATCH = 512
TM_COMBINE = 256
BM_FFN = 256
VMEM_LIMIT = 56 * 1024 * 1024


def _cparams(n_axes=1):
    return pltpu.CompilerParams(
        dimension_semantics=("arbitrary",) * n_axes, vmem_limit_bytes=VMEM_LIMIT)


def _sigmoid(x):
    return 1.0 / (1.0 + jnp.exp(-x))


def _gelu_tanh(x):
    c = 0.7978845608028654
    return x * (0.5 * (1.0 + jnp.tanh(c * (x + 0.044715 * (x * x * x)))))


def _log_sigmoid(x):
    return jnp.minimum(x, 0.0) - jnp.log1p(jnp.exp(-jnp.abs(x)))


def _store_row_tiled(ref, x, row0=0):
    n_rows = x.shape[0]
    for g in range(n_rows // SUBLANES):
        for j in range(ROW_TILES):
            ref[pl.ds((row0 + g * SUBLANES) * ROW_TILES + j, SUBLANES, stride=ROW_TILES), :] = (
                x[g * SUBLANES:(g + 1) * SUBLANES, j * LANES:(j + 1) * LANES])


def _load_row_tiled(ref, n_rows, row0=0):
    groups = []
    for g in range(n_rows // SUBLANES):
        tiles = [ref[pl.ds((row0 + g * SUBLANES) * ROW_TILES + j, SUBLANES, stride=ROW_TILES), :]
                 for j in range(ROW_TILES)]
        groups.append(jnp.concatenate(tiles, axis=1))
    return jnp.concatenate(groups, axis=0)


def _inproj_kernel(x_ref, gain_ref, wm_ref, wg_ref, main_ref, gate_ref):
    x = x_ref[...]
    xn = x * lax.rsqrt(jnp.mean(x * x, axis=-1, keepdims=True) + EPS) * gain_ref[...]
    xb = xn.astype(BF16)
    main_ref[...] = jnp.dot(xb, wm_ref[...], preferred_element_type=F32).astype(BF16)
    gate_ref[...] = jnp.dot(xb, wg_ref[...], preferred_element_type=F32)


def _inproj(x2, gain, w_main, w_gate):
    T = x2.shape[0]
    return pl.pallas_call(
        _inproj_kernel,
        grid=(T // TM_PROJ,),
        in_specs=[
            pl.BlockSpec((TM_PROJ, D_MODEL), lambda i: (i, 0)),
            pl.BlockSpec((1, D_MODEL), lambda i: (0, 0)),
            pl.BlockSpec((D_MODEL, D_MAIN), lambda i: (0, 0)),
            pl.BlockSpec((D_MODEL, LANES), lambda i: (0, 0)),
        ],
        out_specs=[
            pl.BlockSpec((TM_PROJ, D_MAIN), lambda i: (i, 0)),
            pl.BlockSpec((TM_PROJ, LANES), lambda i: (i, 0)),
        ],
        out_shape=[
            jax.ShapeDtypeStruct((T, D_MAIN), BF16),
            jax.ShapeDtypeStruct((T, LANES), F32),
        ],
        compiler_params=_cparams(),
        name="inproj",
    )(x2, gain, w_main, w_gate)


def _mlstm_kernel(qkvo_ref, gates_ref, conv_ref, gbias_ref, ngain_ref, out_ref,
                  xp_ref, ct_ref, n_ref, m_ref, *, batch):
    c = pl.program_id(0)
    L, DH = CHUNK, M_HEAD_DIM

    @pl.when(c == 0)
    def _():
        xp_ref[:, 0:SUBLANES, :] = jnp.zeros((batch, SUBLANES, 2 * D_MLSTM), F32)
        ct_ref[...] = jnp.zeros_like(ct_ref)
        n_ref[...] = jnp.zeros_like(n_ref)
        m_ref[...] = jnp.zeros_like(m_ref)

    row = lax.broadcasted_iota(jnp.int32, (L, L), 0)
    col = lax.broadcasted_iota(jnp.int32, (L, L), 1)
    tril = row >= col
    tril_f = tril.astype(F32)

    for b in range(batch):
        xp_ref[b, SUBLANES:SUBLANES + L, :] = qkvo_ref[b, :, 0:2 * D_MLSTM].astype(F32)
        off = SUBLANES - (CONV_K - 1)
        acc = xp_ref[b, off:off + L, :] * conv_ref[0:1, :]
        for j in range(1, CONV_K):
            acc = acc + xp_ref[b, off + j:off + j + L, :] * conv_ref[j:j + 1, :]
        xp_ref[b, 0:SUBLANES, :] = xp_ref[b, L:L + SUBLANES, :]
        qk = acc * _sigmoid(acc)

        gc = gates_ref[b] + gbias_ref[...]
        lf = _log_sigmoid(gc)
        ball = jnp.dot(tril_f, lf, precision=lax.Precision.HIGHEST,
                       preferred_element_type=F32)
        gc_t = gc.T
        ball_t = ball.T

        for h in range(M_HEADS):
            s = b * M_HEADS + h
            q = qk[:, h * DH:(h + 1) * DH]
            k = qk[:, D_MLSTM + h * DH:D_MLSTM + (h + 1) * DH] * (DH ** -0.5)
            v = qkvo_ref[b, :, 2 * D_MLSTM + h * DH:2 * D_MLSTM + (h + 1) * DH]
            o = qkvo_ref[b, :, 3 * D_MLSTM + h * DH:3 * D_MLSTM + (h + 1) * DH].astype(F32)
            qb = q.astype(BF16)
            kb = k.astype(BF16)

            b_col = ball[:, M_HEADS + h:M_HEADS + h + 1]
            b_row = ball_t[M_HEADS + h:M_HEADS + h + 1, :]
            li_col = gc[:, h:h + 1]
            li_row = gc_t[h:h + 1, :]
            b_last = b_col[L - 1:L, :]
            m_prev = m_ref[s][0:1, 0:1]
            ct_prev = ct_ref[s]
            n_prev = n_ref[s]

            dmat = jnp.where(tril, b_col - b_row + li_row, -jnp.inf)
            a_inter = b_col + m_prev
            m_t = jnp.maximum(a_inter, jnp.max(dmat, axis=-1, keepdims=True))
            qkt = lax.dot_general(qb, kb, (((1,), (1,)), ((), ())), preferred_element_type=F32)
            w = jnp.exp(dmat - m_t) * qkt
            inter = jnp.exp(a_inter - m_t)
            num = (jnp.dot(w.astype(BF16), v, preferred_element_type=F32)
                   + inter * jnp.dot(qb, ct_prev.astype(BF16), preferred_element_type=F32))
            den = (jnp.sum(w, axis=-1, keepdims=True)
                   + inter * jnp.sum(q * n_prev, axis=-1, keepdims=True))
            hh = num * (1.0 / jnp.maximum(jnp.abs(den), jnp.exp(-m_t)))

            mu = jnp.mean(hh, axis=-1, keepdims=True)
            xc = hh - mu
            y = xc * lax.rsqrt(jnp.mean(xc * xc, axis=-1, keepdims=True) + EPS)
            y = y * ngain_ref[:, h * DH:(h + 1) * DH]
            out_ref[b, :, h * DH:(h + 1) * DH] = (_sigmoid(o) * y).astype(out_ref.dtype)

            g_col = b_last - b_col + li_col
            m_loc = jnp.max(g_col, axis=0, keepdims=True)
            w_col = jnp.exp(g_col - m_loc)
            wv = (w_col * v.astype(F32)).astype(BF16)
            ct_loc = jnp.dot(k.T.astype(BF16), wv, preferred_element_type=F32)
            n_loc = jnp.sum(w_col * k, axis=0, keepdims=True)
            m_new = jnp.maximum(b_last + m_prev, m_loc)
            a = jnp.exp(b_last + m_prev - m_new)
            cc = jnp.exp(m_loc - m_new)
            ct_ref[s] = a * ct_prev + cc * ct_loc
            n_ref[s] = a * n_prev + cc * n_loc
            m_ref[s] = jnp.broadcast_to(m_new, (SUBLANES, LANES))


def _mlstm(main3, gates3, conv_qk, gbias, ngain):
    B, S, _ = main3.shape
    nc = S // CHUNK
    return pl.pallas_call(
        functools.partial(_mlstm_kernel, batch=B),
        grid=(nc,),
        in_specs=[
            pl.BlockSpec((B, CHUNK, 4 * D_MLSTM), lambda c: (0, c, 0)),
            pl.BlockSpec((B, CHUNK, LANES), lambda c: (0, c, 0)),
            pl.BlockSpec((CONV_K, 2 * D_MLSTM), lambda c: (0, 0)),
            pl.BlockSpec((1, LANES), lambda c: (0, 0)),
            pl.BlockSpec((1, D_MLSTM), lambda c: (0, 0)),
        ],
        out_specs=pl.BlockSpec((B, CHUNK, D_MLSTM), lambda c: (0, c, 0)),
        out_shape=jax.ShapeDtypeStruct((B, S, D_MLSTM), BF16),
        scratch_shapes=[
            pltpu.VMEM((B, CHUNK + SUBLANES, 2 * D_MLSTM), F32),
            pltpu.VMEM((B * M_HEADS, M_HEAD_DIM, M_HEAD_DIM), F32),
            pltpu.VMEM((B * M_HEADS, 1, M_HEAD_DIM), F32),
            pltpu.VMEM((B * M_HEADS, SUBLANES, LANES), F32),
        ],
        compiler_params=_cparams(),
        name="mlstm",
    )(main3, gates3, conv_qk, gbias, ngain)


def _gmlp_kernel(uv_ref, lng_ref, ws_ref, bs_ref, og_ref, out_ref, prod_ref):
    L = CHUNK
    u = _gelu_tanh(uv_ref[:, 0:D_GMLP].astype(F32))
    vg = _gelu_tanh(uv_ref[:, D_GMLP:2 * D_GMLP].astype(F32))
    mu = jnp.mean(vg, axis=-1, keepdims=True)
    xc = vg - mu
    vg = xc * lax.rsqrt(jnp.mean(xc * xc, axis=-1, keepdims=True) + EPS) * lng_ref[...]
    vb = vg.astype(BF16)
    row = lax.broadcasted_iota(jnp.int32, (L, L), 0)
    col = lax.broadcasted_iota(jnp.int32, (L, L), 1)
    tril = row >= col
    for g in range(G_GROUPS):
        wg = jnp.where(tril, ws_ref[g], 0.0).astype(BF16)
        b_col = bs_ref[:, g:g + 1]
        for j in range(TG_GMLP // L):
            mixed = jnp.dot(wg, vb[j * L:(j + 1) * L, g * G_DIM:(g + 1) * G_DIM],
                            preferred_element_type=F32) + b_col
            prod_ref[j * L:(j + 1) * L, g * G_DIM:(g + 1) * G_DIM] = (
                u[j * L:(j + 1) * L, g * G_DIM:(g + 1) * G_DIM] * mixed)
    p = prod_ref[...]
    y = p * lax.rsqrt(jnp.mean(p * p, axis=-1, keepdims=True) + EPS) * og_ref[...]
    out_ref[...] = y.astype(out_ref.dtype)


def _gmlp(main, ln_gain, w_spatial, b_spatial_t, out_gain):
    T = main.shape[0]
    uv_block = 4 * D_MLSTM // (2 * D_GMLP)
    return pl.pallas_call(
        _gmlp_kernel,
        grid=(T // TG_GMLP,),
        in_specs=[
            pl.BlockSpec((TG_GMLP, 2 * D_GMLP), lambda i: (i, uv_block)),
            pl.BlockSpec((1, D_GMLP), lambda i: (0, 0)),
            pl.BlockSpec((G_GROUPS, CHUNK, CHUNK), lambda i: (0, 0, 0)),
            pl.BlockSpec((CHUNK, G_GROUPS), lambda i: (0, 0)),
            pl.BlockSpec((1, D_GMLP), lambda i: (0, 0)),
        ],
        out_specs=pl.BlockSpec((TG_GMLP, D_GMLP), lambda i: (i, 0)),
        out_shape=jax.ShapeDtypeStruct((T, D_GMLP), BF16),
        scratch_shapes=[pltpu.VMEM((TG_GMLP, D_GMLP), F32)],
        compiler_params=_cparams(),
        name="gmlp",
    )(main, ln_gain, w_spatial, b_spatial_t, out_gain)


def _outproj_kernel(hm_ref, hg_ref, x_ref, wo_ref, g2_ref, wr_ref, br_ref, tril_ref,
                    h_ref, hn_ref, route_ref, cnt_ref, carry_ref):
    i = pl.program_id(0)
    tm = TM_ROUTE

    @pl.when(i == 0)
    def _():
        carry_ref[...] = jnp.zeros_like(carry_ref)

    y = (jnp.dot(hm_ref[...], wo_ref[0:D_MLSTM, :], preferred_element_type=F32)
         + jnp.dot(hg_ref[...], wo_ref[D_MLSTM:D_MODEL, :], preferred_element_type=F32))
    h = x_ref[...] + y
    h_ref[...] = h
    hn = h * lax.rsqrt(jnp.mean(h * h, axis=-1, keepdims=True) + EPS) * g2_ref[...]
    _store_row_tiled(hn_ref, hn)

    logits = jnp.dot(hn.astype(BF16), wr_ref[...], preferred_element_type=F32) + br_ref[...]
    lane = lax.broadcasted_iota(jnp.int32, (tm, LANES), 1).astype(F32)
    l = jnp.where(lane < N_EXPERTS, logits, -jnp.inf)
    top_v, top_i = [], []
    for _ in range(TOP_K):
        mx = jnp.max(l, axis=-1, keepdims=True)
        ix = jnp.min(jnp.where(l == mx, lane, float(LANES)), axis=-1, keepdims=True)
        top_v.append(mx)
        top_i.append(ix)
        l = jnp.where(lane == ix, -jnp.inf, l)
    ex = [jnp.exp(v - top_v[0]) for v in top_v]
    inv = 1.0 / (ex[0] + ex[1] + ex[2] + ex[3])

    onehot = [lane == ix for ix in top_i]
    mask = (onehot[0].astype(F32) + onehot[1].astype(F32)
            + onehot[2].astype(F32) + onehot[3].astype(F32))
    incl = jnp.dot(tril_ref[...], mask.astype(BF16), preferred_element_type=F32)
    base = carry_ref[0:1, :] + incl - mask
    route = jnp.zeros((tm, LANES), F32)
    for k in range(TOP_K):
        rank = jnp.sum(jnp.where(onehot[k], base, 0.0), axis=-1, keepdims=True)
        route = jnp.where(lane == k, top_i[k], route)
        route = jnp.where(lane == TOP_K + k, rank, route)
        route = jnp.where(lane == 2 * TOP_K + k, ex[k] * inv, route)
    route_ref[...] = route
    carry_ref[...] = carry_ref[...] + incl[tm - 1:tm, :]
    cnt_ref[...] = carry_ref[...]


def _outproj(h_m, h_g, x2, w_out, gain2, w_router, b_router, tril):
    T = x2.shape[0]
    tm = TM_ROUTE
    return pl.pallas_call(
        _outproj_kernel,
        grid=(T // tm,),
        in_specs=[
            pl.BlockSpec((tm, D_MLSTM), lambda i: (i, 0)),
            pl.BlockSpec((tm, D_GMLP), lambda i: (i, 0)),
            pl.BlockSpec((tm, D_MODEL), lambda i: (i, 0)),
            pl.BlockSpec((D_MODEL, D_MODEL), lambda i: (0, 0)),
            pl.BlockSpec((1, D_MODEL), lambda i: (0, 0)),
            pl.BlockSpec((D_MODEL, LANES), lambda i: (0, 0)),
            pl.BlockSpec((1, LANES), lambda i: (0, 0)),
            pl.BlockSpec((tm, tm), lambda i: (0, 0)),
        ],
        out_specs=[
            pl.BlockSpec((tm, D_MODEL), lambda i: (i, 0)),
            pl.BlockSpec((tm * ROW_TILES, LANES), lambda i: (i, 0)),
            pl.BlockSpec((tm, LANES), lambda i: (i, 0)),
            pl.BlockSpec((SUBLANES, LANES), lambda i: (0, 0)),
        ],
        out_shape=[
            jax.ShapeDtypeStruct((T, D_MODEL), F32),
            jax.ShapeDtypeStruct((T * ROW_TILES, LANES), F32),
            jax.ShapeDtypeStruct((T, LANES), F32),
            jax.ShapeDtypeStruct((SUBLANES, LANES), F32),
        ],
        scratch_shapes=[pltpu.VMEM((SUBLANES, LANES), F32)],
        compiler_params=_cparams(),
        name="outproj_router",
    )(h_m, h_g, x2, w_out, gain2, w_router, b_router, tril)


def _dest_kernel(route_ref, pstart_ref, dest_ref):
    r = route_ref[...]
    tm = r.shape[0]
    lane = lax.broadcasted_iota(jnp.int32, (tm, LANES), 1).astype(F32)
    ps = pstart_ref[...]
    out = jnp.zeros((tm, LANES), F32)
    for k in range(TOP_K):
        ix = r[:, k:k + 1]
        rank = r[:, TOP_K + k:TOP_K + k + 1]
        d = rank + jnp.sum(jnp.where(lane == ix, ps, 0.0), axis=-1, keepdims=True)
        out = jnp.where(lane == k, d, out)
    dest_ref[...] = out.astype(jnp.int32)


def _dest(route, pstart):
    T = route.shape[0]
    tm = TM_ROUTE
    return pl.pallas_call(
        _dest_kernel,
        grid=(T // tm,),
        in_specs=[
            pl.BlockSpec((tm, LANES), lambda i: (i, 0)),
            pl.BlockSpec((1, LANES), lambda i: (0, 0)),
        ],
        out_specs=pl.BlockSpec((tm, LANES), lambda i: (i, 0)),
        out_shape=jax.ShapeDtypeStruct((T, LANES), jnp.int32),
        compiler_params=_cparams(),
        name="dest",
    )(route, pstart)


def _row_slice(n):
    return pl.ds(pl.multiple_of(n * ROW_TILES, ROW_TILES), ROW_TILES)


def _dispatch_kernel(dest_ref, pend_ref, hn_ref, xs_hbm, zero_ref, sem, zsem):
    i = pl.program_id(0)
    base = i * TD_DISPATCH
    blk_rows = BM_FFN * ROW_TILES

    @pl.when(i == 0)
    def _():
        zero_ref[...] = jnp.zeros_like(zero_ref)

        def zero_copy(e):
            start = pl.multiple_of((pend_ref[e] - BM_FFN) * ROW_TILES, blk_rows)
            return pltpu.make_async_copy(zero_ref, xs_hbm.at[pl.ds(start, blk_rows)], zsem)

        def nonempty(e):
            return pend_ref[e] > (pend_ref[e - 1] if e > 0 else 0)

        nb = xs_hbm.shape[0] // blk_rows

        def tail_copy(j):
            return pltpu.make_async_copy(
                zero_ref, xs_hbm.at[pl.ds((nb - 1 - j) * blk_rows, blk_rows)], zsem)

        def unused(j):
            return (nb - 1 - j) * BM_FFN >= pend_ref[N_EXPERTS - 1]

        for e in range(N_EXPERTS):
            @pl.when(nonempty(e))
            def _():
                zero_copy(e).start()

            @pl.when(unused(e))
            def _():
                tail_copy(e).start()
        for e in range(N_EXPERTS):
            @pl.when(nonempty(e))
            def _():
                zero_copy(e).wait()

            @pl.when(unused(e))
            def _():
                tail_copy(e).wait()

    def row_copy(t, k):
        d = dest_ref[(base + t) * TOP_K + k]
        return pltpu.make_async_copy(hn_ref.at[_row_slice(t)], xs_hbm.at[_row_slice(d)], sem)

    def issue(t, carry):
        for k in range(TOP_K):
            row_copy(t, k).start()
        return carry

    def drain(t, carry):
        for k in range(TOP_K):
            row_copy(t, k).wait()
        return carry

    lax.fori_loop(0, TD_DISPATCH, issue, 0, unroll=8)
    lax.fori_loop(0, TD_DISPATCH, drain, 0, unroll=8)


def _dispatch(dest_flat, pend, hn_rt, nbuf):
    T = hn_rt.shape[0] // ROW_TILES
    return pl.pallas_call(
        _dispatch_kernel,
        grid_spec=pltpu.PrefetchScalarGridSpec(
            num_scalar_prefetch=2,
            grid=(T // TD_DISPATCH,),
            in_specs=[pl.BlockSpec((TD_DISPATCH * ROW_TILES, LANES), lambda i, d, p: (i, 0))],
            out_specs=pl.BlockSpec(memory_space=pl.ANY),
            scratch_shapes=[
                pltpu.VMEM((BM_FFN * ROW_TILES, LANES), F32),
                pltpu.SemaphoreType.DMA,
                pltpu.SemaphoreType.DMA,
            ],
        ),
        out_shape=jax.ShapeDtypeStruct((nbuf * ROW_TILES, LANES), F32),
        compiler_params=_cparams(),
        name="dispatch",
    )(dest_flat, pend, hn_rt)


def _ffn_kernel(be_ref, nv_ref, xs_ref, wup_ref, bup_ref, wdn_ref, bdn_ref, y_ref,
                wup_bf, wdn_bf):
    i = pl.program_id(0)
    e = be_ref[i]
    e_prev = be_ref[jnp.maximum(i - 1, 0)]
    first = jnp.logical_or(i == 0, e != e_prev)
    valid = i < nv_ref[0]

    @pl.when(jnp.logical_and(first, valid))
    def _():
        wup_bf[...] = wup_ref[0].astype(BF16)
        wdn_bf[...] = wdn_ref[0].astype(BF16)

    @pl.when(valid)
    def _():
        xb = _load_row_tiled(xs_ref, BM_FFN).astype(BF16)
        hb = jnp.dot(xb, wup_bf[...], preferred_element_type=F32) + bup_ref[0]
        glu = jnp.minimum(hb[:, 0:D_FF], SWIGLU_LIMIT)
        lin = jnp.clip(hb[:, D_FF:2 * D_FF], -SWIGLU_LIMIT, SWIGLU_LIMIT)
        act = glu * _sigmoid(SWIGLU_ALPHA * glu) * (lin + 1.0)
        y = jnp.dot(act.astype(BF16), wdn_bf[...], preferred_element_type=F32) + bdn_ref[0]
        _store_row_tiled(y_ref, y)

    @pl.when(jnp.logical_not(valid))
    def _():
        y_ref[...] = jnp.zeros_like(y_ref)


def _ffn(block_e, n_valid, xs_rt, w_up, b_up3, w_down, b_down3):
    blk_rows = BM_FFN * ROW_TILES
    nb = xs_rt.shape[0] // blk_rows
    return pl.pallas_call(
        _ffn_kernel,
        grid_spec=pltpu.PrefetchScalarGridSpec(
            num_scalar_prefetch=2,
            grid=(nb,),
            in_specs=[
                pl.BlockSpec((blk_rows, LANES), lambda i, be, nv: (jnp.minimum(i, nv[0] - 1), 0)),
                pl.BlockSpec((1, D_MODEL, 2 * D_FF), lambda i, be, nv: (be[i], 0, 0)),
                pl.BlockSpec((1, 1, 2 * D_FF), lambda i, be, nv: (be[i], 0, 0)),
                pl.BlockSpec((1, D_FF, D_MODEL), lambda i, be, nv: (be[i], 0, 0)),
                pl.BlockSpec((1, 1, D_MODEL), lambda i, be, nv: (be[i], 0, 0)),
            ],
            out_specs=pl.BlockSpec((blk_rows, LANES), lambda i, be, nv: (i, 0)),
            scratch_shapes=[
                pltpu.VMEM((D_MODEL, 2 * D_FF), BF16),
                pltpu.VMEM((D_FF, D_MODEL), BF16),
            ],
        ),
        out_shape=jax.ShapeDtypeStruct(xs_rt.shape, F32),
        compiler_params=_cparams(),
        name="expert_ffn",
    )(block_e, n_valid, xs_rt, w_up, b_up3, w_down, b_down3)


def _combine_kernel(dest_ref, route_ref, h_ref, gain_ref, yb_hbm, out_ref, buf_ref, sem):
    tm = TM_COMBINE
    base = pl.program_id(0) * tm

    def row_copy(t, k):
        d = dest_ref[(base + t) * TOP_K + k]
        return pltpu.make_async_copy(yb_hbm.at[_row_slice(d)], buf_ref.at[_row_slice(k * tm + t)],
                                     sem)

    def issue(t, carry):
        for k in range(TOP_K):
            row_copy(t, k).start()
        return carry

    def drain(t, carry):
        for k in range(TOP_K):
            row_copy(t, k).wait()
        return carry

    lax.fori_loop(0, tm, issue, 0, unroll=8)
    lax.fori_loop(0, tm, drain, 0, unroll=8)

    r = route_ref[...]
    acc = h_ref[...]
    for k in range(TOP_K):
        acc = acc + r[:, 2 * TOP_K + k:2 * TOP_K + k + 1] * _load_row_tiled(buf_ref, tm, k * tm)
    out = acc * lax.rsqrt(jnp.mean(acc * acc, axis=-1, keepdims=True) + EPS) * gain_ref[...]
    out_ref[...] = out


def _combine(dest_flat, route, h, gain, yb_rt):
    T = h.shape[0]
    tm = TM_COMBINE
    return pl.pallas_call(
        _combine_kernel,
        grid_spec=pltpu.PrefetchScalarGridSpec(
            num_scalar_prefetch=1,
            grid=(T // tm,),
            in_specs=[
                pl.BlockSpec((tm, LANES), lambda i, d: (i, 0)),
                pl.BlockSpec((tm, D_MODEL), lambda i, d: (i, 0)),
                pl.BlockSpec((1, D_MODEL), lambda i, d: (0, 0)),
                pl.BlockSpec(memory_space=pl.ANY),
            ],
            out_specs=pl.BlockSpec((tm, D_MODEL), lambda i, d: (i, 0)),
            scratch_shapes=[
                pltpu.VMEM((TOP_K * tm * ROW_TILES, LANES), F32),
                pltpu.SemaphoreType.DMA,
            ],
        ),
        out_shape=jax.ShapeDtypeStruct((T, D_MODEL), F32),
        compiler_params=_cparams(),
        name="combine",
    )(dest_flat, route, h, gain, yb_rt)


def _layer(x2, B, S, norm1_gain, w_in, conv_qk, b_igate, b_fgate, mlstm_norm_gain, gmlp_ln_gain,
           w_spatial, b_spatial, gmlp_out_gain, w_out, norm2_gain, w_router, b_router,
           w_up, b_up, w_down, b_down):
    T = B * S
    n_gate = 2 * M_HEADS
    g0 = 4 * D_MLSTM
    w_main = jnp.concatenate([w_in[:, :g0], w_in[:, g0 + n_gate:]], axis=1).astype(BF16)
    w_gate = jnp.pad(w_in[:, g0:g0 + n_gate], ((0, 0), (0, LANES - n_gate))).astype(BF16)
    gbias = jnp.pad(jnp.concatenate([b_igate, b_fgate]), (0, LANES - n_gate)).reshape(1, LANES)

    main, gates = _inproj(x2, norm1_gain.reshape(1, D_MODEL), w_main, w_gate)
    h_m = _mlstm(main.reshape(B, S, D_MAIN), gates.reshape(B, S, LANES), conv_qk, gbias,
                 mlstm_norm_gain.reshape(1, D_MLSTM)).reshape(T, D_MLSTM)
    h_g = _gmlp(main, gmlp_ln_gain.reshape(1, D_GMLP), w_spatial, b_spatial.T,
                gmlp_out_gain.reshape(1, D_GMLP))

    w_r = jnp.pad(w_router, ((0, 0), (0, LANES - N_EXPERTS))).astype(BF16)
    b_r = jnp.pad(b_router, (0, LANES - N_EXPERTS)).reshape(1, LANES)
    tril = jnp.tril(jnp.ones((TM_ROUTE, TM_ROUTE), BF16))
    h, hn_rt, route, cnt = _outproj(h_m, h_g, x2, w_out.astype(BF16),
                                    norm2_gain.reshape(1, D_MODEL), w_r, b_r, tril)

    counts = cnt[0, :N_EXPERTS].astype(jnp.int32)
    padded = (counts + BM_FFN - 1) // BM_FFN * BM_FFN
    pend = jnp.cumsum(padded)
    pstart = pend - padded
    nb = (T * TOP_K) // BM_FFN + N_EXPERTS
    n_valid = (pend[-1] // BM_FFN).astype(jnp.int32)
    blk = jnp.arange(nb, dtype=jnp.int32)
    first_row = jnp.minimum(blk, n_valid - 1) * BM_FFN
    block_e = jnp.sum((pend[None, :] <= first_row[:, None]).astype(jnp.int32), axis=1)
    block_e = jnp.minimum(block_e, N_EXPERTS - 1)
    pstart_row = jnp.pad(pstart.astype(F32), (0, LANES - N_EXPERTS)).reshape(1, LANES)

    dest = _dest(route, pstart_row)
    dest_flat = dest[:, :TOP_K].reshape(T * TOP_K)

    xs_rt = _dispatch(dest_flat, pend.astype(jnp.int32), hn_rt, nb * BM_FFN)
    yb_rt = _ffn(block_e, n_valid.reshape(1), xs_rt, w_up, b_up.reshape(N_EXPERTS, 1, 2 * D_FF),
                 w_down, b_down.reshape(N_EXPERTS, 1, D_MODEL))
    return dest_flat, route, h, yb_rt


def kernel(x, norm1_gain, w_in, conv_qk, b_igate, b_fgate, mlstm_norm_gain, gmlp_ln_gain,
           w_spatial, b_spatial, gmlp_out_gain, w_out, norm2_gain, w_router, b_router,
           w_up, b_up, w_down, b_down, final_gain):
    B, S, D = x.shape
    depth = norm1_gain.shape[0]
    assert depth == 1 and D == D_MODEL and S % CHUNK == 0
    x2 = x.reshape(B * S, D)
    l = 0
    dest_flat, route, h, yb_rt = _layer(
        x2, B, S, norm1_gain[l], w_in[l], conv_qk[l], b_igate[l], b_fgate[l], mlstm_norm_gain[l],
        gmlp_ln_gain[l], w_spatial[l], b_spatial[l], gmlp_out_gain[l], w_out[l], norm2_gain[l],
        w_router[l], b_router[l], w_up[l], b_up[l], w_down[l], b_down[l])
    out = _combine(dest_flat, route, h, final_gain.reshape(1, D_MODEL), yb_rt)
    return out.reshape(B, S, D)
```

```python
import functools

import jax
import jax.numpy as jnp
from jax import lax
from jax.experimental import pallas as pl
from jax.experimental.pallas import tpu as pltpu

F32 = jnp.float32
BF16 = jnp.bfloat16

D_MODEL = 1024
M_HEADS = 4
M_HEAD_DIM = 128
D_MLSTM = M_HEADS * M_HEAD_DIM
D_GMLP = D_MODEL - D_MLSTM
G_GROUPS = 4
G_DIM = D_GMLP // G_GROUPS
CHUNK = 128
CONV_K = 4
N_EXPERTS = 32
TOP_K = 4
D_FF = D_MODEL
SWIGLU_LIMIT = 7.0
SWIGLU_ALPHA = 1.702
EPS = 1e-6

LANES = 128
SUBLANES = 8
ROW_TILES = D_MODEL // LANES
assert ROW_TILES == SUBLANES
D_MAIN = 4 * D_MLSTM + 2 * D_GMLP
TM_PROJ = 512
TG_GMLP = 512
TM_ROUTE = 512
TD_DISPATCH = 512
TM_COMBINE = 256
BM_FFN = 256
DMA_QUEUES = 2
VMEM_LIMIT = 56 * 1024 * 1024


def _cparams(n_axes=1):
    return pltpu.CompilerParams(
        dimension_semantics=("arbitrary",) * n_axes, vmem_limit_bytes=VMEM_LIMIT)


def _sigmoid(x):
    return 1.0 / (1.0 + jnp.exp(-x))


def _gelu_tanh(x):
    c = 0.7978845608028654
    return x * (0.5 * (1.0 + jnp.tanh(c * (x + 0.044715 * (x * x * x)))))


def _log_sigmoid(x):
    return jnp.minimum(x, 0.0) - jnp.log1p(jnp.exp(-jnp.abs(x)))


def _store_row_tiled(ref, x, row0=0):
    n_rows = x.shape[0]
    for g in range(n_rows // SUBLANES):
        for j in range(ROW_TILES):
            ref[pl.ds((row0 + g * SUBLANES) * ROW_TILES + j, SUBLANES, stride=ROW_TILES), :] = (
                x[g * SUBLANES:(g + 1) * SUBLANES, j * LANES:(j + 1) * LANES])


def _load_row_tiled(ref, n_rows, row0=0):
    groups = []
    for g in range(n_rows // SUBLANES):
        tiles = [ref[pl.ds((row0 + g * SUBLANES) * ROW_TILES + j, SUBLANES, stride=ROW_TILES), :]
                 for j in range(ROW_TILES)]
        groups.append(jnp.concatenate(tiles, axis=1))
    return jnp.concatenate(groups, axis=0)


def _inproj_kernel(x_ref, gain_ref, wm_ref, wg_ref, main_ref, gate_ref):
    x = x_ref[...]
    xn = x * lax.rsqrt(jnp.mean(x * x, axis=-1, keepdims=True) + EPS) * gain_ref[...]
    xb = xn.astype(BF16)
    main_ref[...] = jnp.dot(xb, wm_ref[...], preferred_element_type=F32).astype(BF16)
    gate_ref[...] = jnp.dot(xb, wg_ref[...], preferred_element_type=F32)


def _inproj(x2, gain, w_main, w_gate):
    T = x2.shape[0]
    return pl.pallas_call(
        _inproj_kernel,
        grid=(T // TM_PROJ,),
        in_specs=[
            pl.BlockSpec((TM_PROJ, D_MODEL), lambda i: (i, 0)),
            pl.BlockSpec((1, D_MODEL), lambda i: (0, 0)),
            pl.BlockSpec((D_MODEL, D_MAIN), lambda i: (0, 0)),
            pl.BlockSpec((D_MODEL, LANES), lambda i: (0, 0)),
        ],
        out_specs=[
            pl.BlockSpec((TM_PROJ, D_MAIN), lambda i: (i, 0)),
            pl.BlockSpec((TM_PROJ, LANES), lambda i: (i, 0)),
        ],
        out_shape=[
            jax.ShapeDtypeStruct((T, D_MAIN), BF16),
            jax.ShapeDtypeStruct((T, LANES), F32),
        ],
        compiler_params=_cparams(),
        name="inproj",
    )(x2, gain, w_main, w_gate)


def _mlstm_kernel(qkvo_ref, gates_ref, conv_ref, gbias_ref, ngain_ref, out_ref,
                  xp_ref, ct_ref, n_ref, m_ref, *, batch):
    c = pl.program_id(0)
    L, DH = CHUNK, M_HEAD_DIM

    @pl.when(c == 0)
    def _():
        xp_ref[:, 0:SUBLANES, :] = jnp.zeros((batch, SUBLANES, 2 * D_MLSTM), F32)
        ct_ref[...] = jnp.zeros_like(ct_ref)
        n_ref[...] = jnp.zeros_like(n_ref)
        m_ref[...] = jnp.zeros_like(m_ref)

    row = lax.broadcasted_iota(jnp.int32, (L, L), 0)
    col = lax.broadcasted_iota(jnp.int32, (L, L), 1)
    tril = row >= col
    tril_f = tril.astype(F32)

    for b in range(batch):
        xp_ref[b, SUBLANES:SUBLANES + L, :] = qkvo_ref[b, :, 0:2 * D_MLSTM].astype(F32)
        off = SUBLANES - (CONV_K - 1)
        acc = xp_ref[b, off:off + L, :] * conv_ref[0:1, :]
        for j in range(1, CONV_K):
            acc = acc + xp_ref[b, off + j:off + j + L, :] * conv_ref[j:j + 1, :]
        xp_ref[b, 0:SUBLANES, :] = xp_ref[b, L:L + SUBLANES, :]
        qk = acc * _sigmoid(acc)

        gc = gates_ref[b] + gbias_ref[...]
        lf = _log_sigmoid(gc)
        ball = jnp.dot(tril_f, lf, precision=lax.Precision.HIGHEST,
                       preferred_element_type=F32)
        gc_t = gc.T
        ball_t = ball.T

        for h in range(M_HEADS):
            s = b * M_HEADS + h
            q = qk[:, h * DH:(h + 1) * DH]
            k = qk[:, D_MLSTM + h * DH:D_MLSTM + (h + 1) * DH] * (DH ** -0.5)
            v = qkvo_ref[b, :, 2 * D_MLSTM + h * DH:2 * D_MLSTM + (h + 1) * DH]
            o = qkvo_ref[b, :, 3 * D_MLSTM + h * DH:3 * D_MLSTM + (h + 1) * DH].astype(F32)
            qb = q.astype(BF16)
            kb = k.astype(BF16)

            b_col = ball[:, M_HEADS + h:M_HEADS + h + 1]
            b_row = ball_t[M_HEADS + h:M_HEADS + h + 1, :]
            li_col = gc[:, h:h + 1]
            li_row = gc_t[h:h + 1, :]
            b_last = b_col[L - 1:L, :]
            m_prev = m_ref[s][0:1, 0:1]
            ct_prev = ct_ref[s]
            n_prev = n_ref[s]

            dmat = jnp.where(tril, b_col - b_row + li_row, -jnp.inf)
            a_inter = b_col + m_prev
            m_t = jnp.maximum(a_inter, jnp.max(dmat, axis=-1, keepdims=True))
            qkt = lax.dot_general(qb, kb, (((1,), (1,)), ((), ())), preferred_element_type=F32)
            w = jnp.exp(dmat - m_t) * qkt
            inter = jnp.exp(a_inter - m_t)
            num = (jnp.dot(w.astype(BF16), v, preferred_element_type=F32)
                   + inter * jnp.dot(qb, ct_prev.astype(BF16), preferred_element_type=F32))
            den = (jnp.sum(w, axis=-1, keepdims=True)
                   + inter * jnp.sum(q * n_prev, axis=-1, keepdims=True))
            hh = num * (1.0 / jnp.maximum(jnp.abs(den), jnp.exp(-m_t)))

            mu = jnp.mean(hh, axis=-1, keepdims=True)
            xc = hh - mu
            y = xc * lax.rsqrt(jnp.mean(xc * xc, axis=-1, keepdims=True) + EPS)
            y = y * ngain_ref[:, h * DH:(h + 1) * DH]
            out_ref[b, :, h * DH:(h + 1) * DH] = (_sigmoid(o) * y).astype(out_ref.dtype)

            g_col = b_last - b_col + li_col
            m_loc = jnp.max(g_col, axis=0, keepdims=True)
            w_col = jnp.exp(g_col - m_loc)
            wv = (w_col * v.astype(F32)).astype(BF16)
            ct_loc = jnp.dot(k.T.astype(BF16), wv, preferred_element_type=F32)
            n_loc = jnp.sum(w_col * k, axis=0, keepdims=True)
            m_new = jnp.maximum(b_last + m_prev, m_loc)
            a = jnp.exp(b_last + m_prev - m_new)
            cc = jnp.exp(m_loc - m_new)
            ct_ref[s] = a * ct_prev + cc * ct_loc
            n_ref[s] = a * n_prev + cc * n_loc
            m_ref[s] = jnp.broadcast_to(m_new, (SUBLANES, LANES))


def _mlstm(main3, gates3, conv_qk, gbias, ngain):
    B, S, _ = main3.shape
    nc = S // CHUNK
    return pl.pallas_call(
        functools.partial(_mlstm_kernel, batch=B),
        grid=(nc,),
        in_specs=[
            pl.BlockSpec((B, CHUNK, 4 * D_MLSTM), lambda c: (0, c, 0)),
            pl.BlockSpec((B, CHUNK, LANES), lambda c: (0, c, 0)),
            pl.BlockSpec((CONV_K, 2 * D_MLSTM), lambda c: (0, 0)),
            pl.BlockSpec((1, LANES), lambda c: (0, 0)),
            pl.BlockSpec((1, D_MLSTM), lambda c: (0, 0)),
        ],
        out_specs=pl.BlockSpec((B, CHUNK, D_MLSTM), lambda c: (0, c, 0)),
        out_shape=jax.ShapeDtypeStruct((B, S, D_MLSTM), BF16),
        scratch_shapes=[
            pltpu.VMEM((B, CHUNK + SUBLANES, 2 * D_MLSTM), F32),
            pltpu.VMEM((B * M_HEADS, M_HEAD_DIM, M_HEAD_DIM), F32),
            pltpu.VMEM((B * M_HEADS, 1, M_HEAD_DIM), F32),
            pltpu.VMEM((B * M_HEADS, SUBLANES, LANES), F32),
        ],
        compiler_params=_cparams(),
        name="mlstm",
    )(main3, gates3, conv_qk, gbias, ngain)


def _gmlp_kernel(uv_ref, lng_ref, ws_ref, bs_ref, og_ref, out_ref, prod_ref):
    L = CHUNK
    u = _gelu_tanh(uv_ref[:, 0:D_GMLP].astype(F32))
    vg = _gelu_tanh(uv_ref[:, D_GMLP:2 * D_GMLP].astype(F32))
    mu = jnp.mean(vg, axis=-1, keepdims=True)
    xc = vg - mu
    vg = xc * lax.rsqrt(jnp.mean(xc * xc, axis=-1, keepdims=True) + EPS) * lng_ref[...]
    vb = vg.astype(BF16)
    row = lax.broadcasted_iota(jnp.int32, (L, L), 0)
    col = lax.broadcasted_iota(jnp.int32, (L, L), 1)
    tril = row >= col
    for g in range(G_GROUPS):
        wg = jnp.where(tril, ws_ref[g], 0.0).astype(BF16)
        b_col = bs_ref[:, g:g + 1]
        for j in range(TG_GMLP // L):
            mixed = jnp.dot(wg, vb[j * L:(j + 1) * L, g * G_DIM:(g + 1) * G_DIM],
                            preferred_element_type=F32) + b_col
            prod_ref[j * L:(j + 1) * L, g * G_DIM:(g + 1) * G_DIM] = (
                u[j * L:(j + 1) * L, g * G_DIM:(g + 1) * G_DIM] * mixed)
    p = prod_ref[...]
    y = p * lax.rsqrt(jnp.mean(p * p, axis=-1, keepdims=True) + EPS) * og_ref[...]
    out_ref[...] = y.astype(out_ref.dtype)


def _gmlp(main, ln_gain, w_spatial, b_spatial_t, out_gain):
    T = main.shape[0]
    uv_block = 4 * D_MLSTM // (2 * D_GMLP)
    return pl.pallas_call(
        _gmlp_kernel,
        grid=(T // TG_GMLP,),
        in_specs=[
            pl.BlockSpec((TG_GMLP, 2 * D_GMLP), lambda i: (i, uv_block)),
            pl.BlockSpec((1, D_GMLP), lambda i: (0, 0)),
            pl.BlockSpec((G_GROUPS, CHUNK, CHUNK), lambda i: (0, 0, 0)),
            pl.BlockSpec((CHUNK, G_GROUPS), lambda i: (0, 0)),
            pl.BlockSpec((1, D_GMLP), lambda i: (0, 0)),
        ],
        out_specs=pl.BlockSpec((TG_GMLP, D_GMLP), lambda i: (i, 0)),
        out_shape=jax.ShapeDtypeStruct((T, D_GMLP), BF16),
        scratch_shapes=[pltpu.VMEM((TG_GMLP, D_GMLP), F32)],
        compiler_params=_cparams(),
        name="gmlp",
    )(main, ln_gain, w_spatial, b_spatial_t, out_gain)


def _outproj_kernel(hm_ref, hg_ref, x_ref, wo_ref, g2_ref, wr_ref, br_ref, tril_ref,
                    h_ref, hn_ref, route_ref, cnt_ref, carry_ref):
    i = pl.program_id(0)
    tm = TM_ROUTE

    @pl.when(i == 0)
    def _():
        carry_ref[...] = jnp.zeros_like(carry_ref)

    y = (jnp.dot(hm_ref[...], wo_ref[0:D_MLSTM, :], preferred_element_type=F32)
         + jnp.dot(hg_ref[...], wo_ref[D_MLSTM:D_MODEL, :], preferred_element_type=F32))
    h = x_ref[...] + y
    h_ref[...] = h
    hn = h * lax.rsqrt(jnp.mean(h * h, axis=-1, keepdims=True) + EPS) * g2_ref[...]
    _store_row_tiled(hn_ref, hn)

    logits = jnp.dot(hn.astype(BF16), wr_ref[...], preferred_element_type=F32) + br_ref[...]
    lane = lax.broadcasted_iota(jnp.int32, (tm, LANES), 1).astype(F32)
    l = jnp.where(lane < N_EXPERTS, logits, -jnp.inf)
    top_v, top_i = [], []
    for _ in range(TOP_K):
        mx = jnp.max(l, axis=-1, keepdims=True)
        ix = jnp.min(jnp.where(l == mx, lane, float(LANES)), axis=-1, keepdims=True)
        top_v.append(mx)
        top_i.append(ix)
        l = jnp.where(lane == ix, -jnp.inf, l)
    ex = [jnp.exp(v - top_v[0]) for v in top_v]
    inv = 1.0 / (ex[0] + ex[1] + ex[2] + ex[3])

    onehot = [lane == ix for ix in top_i]
    mask = (onehot[0].astype(F32) + onehot[1].astype(F32)
            + onehot[2].astype(F32) + onehot[3].astype(F32))
    incl = jnp.dot(tril_ref[...], mask.astype(BF16), preferred_element_type=F32)
    base = carry_ref[0:1, :] + incl - mask
    route = jnp.zeros((tm, LANES), F32)
    for k in range(TOP_K):
        rank = jnp.sum(jnp.where(onehot[k], base, 0.0), axis=-1, keepdims=True)
        route = jnp.where(lane == k, top_i[k], route)
        route = jnp.where(lane == TOP_K + k, rank, route)
        route = jnp.where(lane == 2 * TOP_K + k, ex[k] * inv, route)
    route_ref[...] = route
    carry_ref[...] = carry_ref[...] + incl[tm - 1:tm, :]
    cnt_ref[...] = carry_ref[...]


def _outproj(h_m, h_g, x2, w_out, gain2, w_router, b_router, tril):
    T = x2.shape[0]
    tm = TM_ROUTE
    return pl.pallas_call(
        _outproj_kernel,
        grid=(T // tm,),
        in_specs=[
            pl.BlockSpec((tm, D_MLSTM), lambda i: (i, 0)),
            pl.BlockSpec((tm, D_GMLP), lambda i: (i, 0)),
            pl.BlockSpec((tm, D_MODEL), lambda i: (i, 0)),
            pl.BlockSpec((D_MODEL, D_MODEL), lambda i: (0, 0)),
            pl.BlockSpec((1, D_MODEL), lambda i: (0, 0)),
            pl.BlockSpec((D_MODEL, LANES), lambda i: (0, 0)),
            pl.BlockSpec((1, LANES), lambda i: (0, 0)),
            pl.BlockSpec((tm, tm), lambda i: (0, 0)),
        ],
        out_specs=[
            pl.BlockSpec((tm, D_MODEL), lambda i: (i, 0)),
            pl.BlockSpec((tm * ROW_TILES, LANES), lambda i: (i, 0)),
            pl.BlockSpec((tm, LANES), lambda i: (i, 0)),
            pl.BlockSpec((SUBLANES, LANES), lambda i: (0, 0)),
        ],
        out_shape=[
            jax.ShapeDtypeStruct((T, D_MODEL), F32),
            jax.ShapeDtypeStruct((T * ROW_TILES, LANES), F32),
            jax.ShapeDtypeStruct((T, LANES), F32),
            jax.ShapeDtypeStruct((SUBLANES, LANES), F32),
        ],
        scratch_shapes=[pltpu.VMEM((SUBLANES, LANES), F32)],
        compiler_params=_cparams(),
        name="outproj_router",
    )(h_m, h_g, x2, w_out, gain2, w_router, b_router, tril)


def _dest_kernel(route_ref, pstart_ref, dest_ref):
    r = route_ref[...]
    tm = r.shape[0]
    lane = lax.broadcasted_iota(jnp.int32, (tm, LANES), 1).astype(F32)
    ps = pstart_ref[...]
    out = jnp.zeros((tm, LANES), F32)
    for k in range(TOP_K):
        ix = r[:, k:k + 1]
        rank = r[:, TOP_K + k:TOP_K + k + 1]
        d = rank + jnp.sum(jnp.where(lane == ix, ps, 0.0), axis=-1, keepdims=True)
        out = jnp.where(lane == k, d, out)
    dest_ref[...] = out.astype(jnp.int32)


def _dest(route, pstart):
    T = route.shape[0]
    tm = TM_ROUTE
    return pl.pallas_call(
        _dest_kernel,
        grid=(T // tm,),
        in_specs=[
            pl.BlockSpec((tm, LANES), lambda i: (i, 0)),
            pl.BlockSpec((1, LANES), lambda i: (0, 0)),
        ],
        out_specs=pl.BlockSpec((tm, LANES), lambda i: (i, 0)),
        out_shape=jax.ShapeDtypeStruct((T, LANES), jnp.int32),
        compiler_params=_cparams(),
        name="dest",
    )(route, pstart)


def _row_slice(n):
    return pl.ds(pl.multiple_of(n * ROW_TILES, ROW_TILES), ROW_TILES)


def _dispatch_kernel(dest_ref, pend_ref, hn_ref, xs_hbm, zero_ref, sem, zsem):
    i = pl.program_id(0)
    base = i * TD_DISPATCH
    blk_rows = BM_FFN * ROW_TILES

    @pl.when(i == 0)
    def _():
        zero_ref[...] = jnp.zeros_like(zero_ref)

        def zero_copy(e):
            start = pl.multiple_of((pend_ref[e] - BM_FFN) * ROW_TILES, blk_rows)
            return pltpu.make_async_copy(zero_ref, xs_hbm.at[pl.ds(start, blk_rows)], zsem)

        def nonempty(e):
            return pend_ref[e] > (pend_ref[e - 1] if e > 0 else 0)

        nb = xs_hbm.shape[0] // blk_rows

        def tail_copy(j):
            return pltpu.make_async_copy(
                zero_ref, xs_hbm.at[pl.ds((nb - 1 - j) * blk_rows, blk_rows)], zsem)

        def unused(j):
            return (nb - 1 - j) * BM_FFN >= pend_ref[N_EXPERTS - 1]

        for e in range(N_EXPERTS):
            @pl.when(nonempty(e))
            def _():
                zero_copy(e).start()

            @pl.when(unused(e))
            def _():
                tail_copy(e).start()
        for e in range(N_EXPERTS):
            @pl.when(nonempty(e))
            def _():
                zero_copy(e).wait()

            @pl.when(unused(e))
            def _():
                tail_copy(e).wait()

    def row_copy(t, k):
        d = dest_ref[(base + t) * TOP_K + k]
        return pltpu.make_async_copy(hn_ref.at[_row_slice(t)], xs_hbm.at[_row_slice(d)], sem)

    def issue(t, carry):
        for k in range(TOP_K):
            row_copy(t, k).start(priority=k % DMA_QUEUES)
        return carry

    def drain(t, carry):
        for k in range(TOP_K):
            row_copy(t, k).wait()
        return carry

    lax.fori_loop(0, TD_DISPATCH, issue, 0, unroll=8)
    lax.fori_loop(0, TD_DISPATCH, drain, 0, unroll=8)


def _dispatch(dest_flat, pend, hn_rt, nbuf):
    T = hn_rt.shape[0] // ROW_TILES
    return pl.pallas_call(
        _dispatch_kernel,
        grid_spec=pltpu.PrefetchScalarGridSpec(
            num_scalar_prefetch=2,
            grid=(T // TD_DISPATCH,),
            in_specs=[pl.BlockSpec((TD_DISPATCH * ROW_TILES, LANES), lambda i, d, p: (i, 0))],
            out_specs=pl.BlockSpec(memory_space=pl.ANY),
            scratch_shapes=[
                pltpu.VMEM((BM_FFN * ROW_TILES, LANES), F32),
                pltpu.SemaphoreType.DMA,
                pltpu.SemaphoreType.DMA,
            ],
        ),
        out_shape=jax.ShapeDtypeStruct((nbuf * ROW_TILES, LANES), F32),
        compiler_params=_cparams(),
        name="dispatch",
    )(dest_flat, pend, hn_rt)


def _ffn_kernel(bstart_ref, bcnt_ref, xs_hbm, wup_ref, bup_ref, wdn_ref, bdn_ref, y_hbm,
                xbuf, ybuf, wup_bf, wdn_bf, in_sem, out_sem):
    e = pl.program_id(0)
    blk_rows = BM_FFN * ROW_TILES
    b0 = bstart_ref[e]
    cnt = bcnt_ref[e]

    def block_rows(j):
        return pl.ds(pl.multiple_of(j * blk_rows, blk_rows), blk_rows)

    def in_copy(j, slot):
        return pltpu.make_async_copy(xs_hbm.at[block_rows(j)], xbuf.at[slot], in_sem.at[slot])

    def out_copy(j, slot):
        return pltpu.make_async_copy(ybuf.at[slot], y_hbm.at[block_rows(j)], out_sem.at[slot])

    @pl.when(cnt > 0)
    def _():
        in_copy(b0, 0).start()
        wup_bf[...] = wup_ref[0].astype(BF16)
        wdn_bf[...] = wdn_ref[0].astype(BF16)

        def body(j, carry):
            slot = j % 2
            in_copy(b0 + j, slot).wait()

            @pl.when(j + 1 < cnt)
            def _():
                in_copy(b0 + j + 1, 1 - slot).start()

            @pl.when(j >= 2)
            def _():
                out_copy(b0 + j - 2, slot).wait()

            xb = _load_row_tiled(xbuf.at[slot], BM_FFN).astype(BF16)
            hb = jnp.dot(xb, wup_bf[...], preferred_element_type=F32) + bup_ref[0]
            glu = jnp.minimum(hb[:, 0:D_FF], SWIGLU_LIMIT)
            lin = jnp.clip(hb[:, D_FF:2 * D_FF], -SWIGLU_LIMIT, SWIGLU_LIMIT)
            act = glu * _sigmoid(SWIGLU_ALPHA * glu) * (lin + 1.0)
            y = jnp.dot(act.astype(BF16), wdn_bf[...], preferred_element_type=F32) + bdn_ref[0]
            _store_row_tiled(ybuf.at[slot], y)
            out_copy(b0 + j, slot).start()
            return carry

        lax.fori_loop(0, cnt, body, 0)

        @pl.when(cnt >= 2)
        def _():
            out_copy(b0 + cnt - 2, cnt % 2).wait()

        out_copy(b0 + cnt - 1, (cnt - 1) % 2).wait()

    @pl.when(e == N_EXPERTS - 1)
    def _():
        nb = y_hbm.shape[0] // blk_rows
        n_valid = b0 + cnt
        ybuf[0] = jnp.zeros((blk_rows, LANES), F32)
        for j in range(N_EXPERTS):
            @pl.when(nb - 1 - j >= n_valid)
            def _():
                out_copy(nb - 1 - j, 0).start()
        for j in range(N_EXPERTS):
            @pl.when(nb - 1 - j >= n_valid)
            def _():
                out_copy(nb - 1 - j, 0).wait()


def _ffn(bstart, bcnt, xs_rt, w_up, b_up3, w_down, b_down3):
    blk_rows = BM_FFN * ROW_TILES
    wspec = lambda shape: pl.BlockSpec(shape, lambda e, bs, bc: (e, 0, 0))
    return pl.pallas_call(
        _ffn_kernel,
        grid_spec=pltpu.PrefetchScalarGridSpec(
            num_scalar_prefetch=2,
            grid=(N_EXPERTS,),
            in_specs=[
                pl.BlockSpec(memory_space=pl.ANY),
                wspec((1, D_MODEL, 2 * D_FF)),
                wspec((1, 1, 2 * D_FF)),
                wspec((1, D_FF, D_MODEL)),
                wspec((1, 1, D_MODEL)),
            ],
            out_specs=pl.BlockSpec(memory_space=pl.ANY),
            scratch_shapes=[
                pltpu.VMEM((2, blk_rows, LANES), F32),
                pltpu.VMEM((2, blk_rows, LANES), F32),
                pltpu.VMEM((D_MODEL, 2 * D_FF), BF16),
                pltpu.VMEM((D_FF, D_MODEL), BF16),
                pltpu.SemaphoreType.DMA((2,)),
                pltpu.SemaphoreType.DMA((2,)),
            ],
        ),
        out_shape=jax.ShapeDtypeStruct(xs_rt.shape, F32),
        compiler_params=_cparams(),
        name="expert_ffn",
    )(bstart, bcnt, xs_rt, w_up, b_up3, w_down, b_down3)


def _combine_kernel(dest_ref, route_ref, h_ref, gain_ref, yb_hbm, out_ref, buf_ref, sem):
    tm = TM_COMBINE
    base = pl.program_id(0) * tm

    def row_copy(t, k):
        d = dest_ref[(base + t) * TOP_K + k]
        return pltpu.make_async_copy(yb_hbm.at[_row_slice(d)], buf_ref.at[_row_slice(k * tm + t)],
                                     sem)

    def issue(t, carry):
        for k in range(TOP_K):
            row_copy(t, k).start(priority=k % DMA_QUEUES)
        return carry

    def drain(t, carry):
        for k in range(TOP_K):
            row_copy(t, k).wait()
        return carry

    lax.fori_loop(0, tm, issue, 0, unroll=8)
    lax.fori_loop(0, tm, drain, 0, unroll=8)

    r = route_ref[...]
    acc = h_ref[...]
    for k in range(TOP_K):
        acc = acc + r[:, 2 * TOP_K + k:2 * TOP_K + k + 1] * _load_row_tiled(buf_ref, tm, k * tm)
    out = acc * lax.rsqrt(jnp.mean(acc * acc, axis=-1, keepdims=True) + EPS) * gain_ref[...]
    out_ref[...] = out


def _combine(dest_flat, route, h, gain, yb_rt):
    T = h.shape[0]
    tm = TM_COMBINE
    return pl.pallas_call(
        _combine_kernel,
        grid_spec=pltpu.PrefetchScalarGridSpec(
            num_scalar_prefetch=1,
            grid=(T // tm,),
            in_specs=[
                pl.BlockSpec((tm, LANES), lambda i, d: (i, 0)),
                pl.BlockSpec((tm, D_MODEL), lambda i, d: (i, 0)),
                pl.BlockSpec((1, D_MODEL), lambda i, d: (0, 0)),
                pl.BlockSpec(memory_space=pl.ANY),
            ],
            out_specs=pl.BlockSpec((tm, D_MODEL), lambda i, d: (i, 0)),
            scratch_shapes=[
                pltpu.VMEM((TOP_K * tm * ROW_TILES, LANES), F32),
                pltpu.SemaphoreType.DMA,
            ],
        ),
        out_shape=jax.ShapeDtypeStruct((T, D_MODEL), F32),
        compiler_params=_cparams(),
        name="combine",
    )(dest_flat, route, h, gain, yb_rt)


def _layer(x2, B, S, norm1_gain, w_in, conv_qk, b_igate, b_fgate, mlstm_norm_gain, gmlp_ln_gain,
           w_spatial, b_spatial, gmlp_out_gain, w_out, norm2_gain, w_router, b_router,
           w_up, b_up, w_down, b_down):
    T = B * S
    n_gate = 2 * M_HEADS
    g0 = 4 * D_MLSTM
    w_main = jnp.concatenate([w_in[:, :g0], w_in[:, g0 + n_gate:]], axis=1).astype(BF16)
    w_gate = jnp.pad(w_in[:, g0:g0 + n_gate], ((0, 0), (0, LANES - n_gate))).astype(BF16)
    gbias = jnp.pad(jnp.concatenate([b_igate, b_fgate]), (0, LANES - n_gate)).reshape(1, LANES)

    main, gates = _inproj(x2, norm1_gain.reshape(1, D_MODEL), w_main, w_gate)
    h_m = _mlstm(main.reshape(B, S, D_MAIN), gates.reshape(B, S, LANES), conv_qk, gbias,
                 mlstm_norm_gain.reshape(1, D_MLSTM)).reshape(T, D_MLSTM)
    h_g = _gmlp(main, gmlp_ln_gain.reshape(1, D_GMLP), w_spatial, b_spatial.T,
                gmlp_out_gain.reshape(1, D_GMLP))

    w_r = jnp.pad(w_router, ((0, 0), (0, LANES - N_EXPERTS))).astype(BF16)
    b_r = jnp.pad(b_router, (0, LANES - N_EXPERTS)).reshape(1, LANES)
    tril = jnp.tril(jnp.ones((TM_ROUTE, TM_ROUTE), BF16))
    h, hn_rt, route, cnt = _outproj(h_m, h_g, x2, w_out.astype(BF16),
                                    norm2_gain.reshape(1, D_MODEL), w_r, b_r, tril)

    counts = cnt[0, :N_EXPERTS].astype(jnp.int32)
    padded = (counts + BM_FFN - 1) // BM_FFN * BM_FFN
    pend = jnp.cumsum(padded)
    pstart = pend - padded
    nb = (T * TOP_K) // BM_FFN + N_EXPERTS
    pstart_row = jnp.pad(pstart.astype(F32), (0, LANES - N_EXPERTS)).reshape(1, LANES)

    dest = _dest(route, pstart_row)
    dest_flat = dest[:, :TOP_K].reshape(T * TOP_K)

    xs_rt = _dispatch(dest_flat, pend.astype(jnp.int32), hn_rt, nb * BM_FFN)
    yb_rt = _ffn((pstart // BM_FFN).astype(jnp.int32), (padded // BM_FFN).astype(jnp.int32), xs_rt,
                 w_up, b_up.reshape(N_EXPERTS, 1, 2 * D_FF), w_down,
                 b_down.reshape(N_EXPERTS, 1, D_MODEL))
    return dest_flat, route, h, yb_rt


def kernel(x, norm1_gain, w_in, conv_qk, b_igate, b_fgate, mlstm_norm_gain, gmlp_ln_gain,
           w_spatial, b_spatial, gmlp_out_gain, w_out, norm2_gain, w_router, b_router,
           w_up, b_up, w_down, b_down, final_gain):
    B, S, D = x.shape
    depth = norm1_gain.shape[0]
    assert depth == 1 and D == D_MODEL and S % CHUNK == 0
    x2 = x.reshape(B * S, D)
    l = 0
    dest_flat, route, h, yb_rt = _layer(
        x2, B, S, norm1_gain[l], w_in[l], conv_qk[l], b_igate[l], b_fgate[l], mlstm_norm_gain[l],
        gmlp_ln_gain[l], w_spatial[l], b_spatial[l], gmlp_out_gain[l], w_out[l], norm2_gain[l],
        w_router[l], b_router[l], w_up[l], b_up[l], w_down[l], b_down[l])
    out = _combine(dest_flat, route, h, final_gain.reshape(1, D_MODEL), yb_rt)
    return out.reshape(B, S, D)
```

```python
import functools

import jax
import jax.numpy as jnp
from jax import lax
from jax.experimental import pallas as pl
from jax.experimental.pallas import tpu as pltpu

F32 = jnp.float32
BF16 = jnp.bfloat16

D_MODEL = 1024
M_HEADS = 4
M_HEAD_DIM = 128
D_MLSTM = M_HEADS * M_HEAD_DIM
D_GMLP = D_MODEL - D_MLSTM
G_GROUPS = 4
G_DIM = D_GMLP // G_GROUPS
CHUNK = 128
CONV_K = 4
N_EXPERTS = 32
TOP_K = 4
D_FF = D_MODEL
SWIGLU_LIMIT = 7.0
SWIGLU_ALPHA = 1.702
EPS = 1e-6

LANES = 128
SUBLANES = 8
ROW_TILES = D_MODEL // LANES
assert ROW_TILES == SUBLANES
D_MAIN = 4 * D_MLSTM + 2 * D_GMLP
TM_PROJ = 512
TG_GMLP = 512
TM_ROUTE = 512
TD_DISPATCH = 512
TM_COMBINE = 256
BM_FFN = 256
DMA_QUEUES = 2
VMEM_LIMIT = 56 * 1024 * 1024


def _cparams(n_axes=1):
    return pltpu.CompilerParams(
        dimension_semantics=("arbitrary",) * n_axes, vmem_limit_bytes=VMEM_LIMIT)


def _sigmoid(x):
    return 1.0 / (1.0 + jnp.exp(-x))


def _gelu_tanh(x):
    c = 0.7978845608028654
    return x * (0.5 * (1.0 + jnp.tanh(c * (x + 0.044715 * (x * x * x)))))


def _log_sigmoid(x):
    return jnp.minimum(x, 0.0) - jnp.log1p(jnp.exp(-jnp.abs(x)))


def _store_row_tiled(ref, x, row0=0):
    n_rows = x.shape[0]
    for g in range(n_rows // SUBLANES):
        for j in range(ROW_TILES):
            ref[pl.ds((row0 + g * SUBLANES) * ROW_TILES + j, SUBLANES, stride=ROW_TILES), :] = (
                x[g * SUBLANES:(g + 1) * SUBLANES, j * LANES:(j + 1) * LANES])


def _load_row_tiled(ref, n_rows, row0=0):
    groups = []
    for g in range(n_rows // SUBLANES):
        tiles = [ref[pl.ds((row0 + g * SUBLANES) * ROW_TILES + j, SUBLANES, stride=ROW_TILES), :]
                 for j in range(ROW_TILES)]
        groups.append(jnp.concatenate(tiles, axis=1))
    return jnp.concatenate(groups, axis=0)


def _inproj_kernel(x_ref, gain_ref, wm_ref, wg_ref, main_ref, gate_ref):
    x = x_ref[...]
    xn = x * lax.rsqrt(jnp.mean(x * x, axis=-1, keepdims=True) + EPS) * gain_ref[...]
    xb = xn.astype(BF16)
    main_ref[...] = jnp.dot(xb, wm_ref[...], preferred_element_type=F32).astype(BF16)
    gate_ref[...] = jnp.dot(xb, wg_ref[...], preferred_element_type=F32)


def _inproj(x2, gain, w_main, w_gate):
    T = x2.shape[0]
    return pl.pallas_call(
        _inproj_kernel,
        grid=(T // TM_PROJ,),
        in_specs=[
            pl.BlockSpec((TM_PROJ, D_MODEL), lambda i: (i, 0)),
            pl.BlockSpec((1, D_MODEL), lambda i: (0, 0)),
            pl.BlockSpec((D_MODEL, D_MAIN), lambda i: (0, 0)),
            pl.BlockSpec((D_MODEL, LANES), lambda i: (0, 0)),
        ],
        out_specs=[
            pl.BlockSpec((TM_PROJ, D_MAIN), lambda i: (i, 0)),
            pl.BlockSpec((TM_PROJ, LANES), lambda i: (i, 0)),
        ],
        out_shape=[
            jax.ShapeDtypeStruct((T, D_MAIN), BF16),
            jax.ShapeDtypeStruct((T, LANES), F32),
        ],
        compiler_params=_cparams(),
        name="inproj",
    )(x2, gain, w_main, w_gate)


def _mlstm_kernel(qkvo_ref, gates_ref, conv_ref, gbias_ref, ngain_ref, out_ref,
                  xp_ref, ct_ref, n_ref, m_ref, *, batch):
    c = pl.program_id(0)
    L, DH = CHUNK, M_HEAD_DIM

    @pl.when(c == 0)
    def _():
        xp_ref[:, 0:SUBLANES, :] = jnp.zeros((batch, SUBLANES, 2 * D_MLSTM), F32)
        ct_ref[...] = jnp.zeros_like(ct_ref)
        n_ref[...] = jnp.zeros_like(n_ref)
        m_ref[...] = jnp.zeros_like(m_ref)

    row = lax.broadcasted_iota(jnp.int32, (L, L), 0)
    col = lax.broadcasted_iota(jnp.int32, (L, L), 1)
    tril = row >= col
    tril_f = tril.astype(F32)
    chains = [(b, h) for b in range(batch) for h in range(M_HEADS)]


    qk, gc, ball, gc_t, ball_t = [], [], [], [], []
    for b in range(batch):
        xp_ref[b, SUBLANES:SUBLANES + L, :] = qkvo_ref[b, :, 0:2 * D_MLSTM].astype(F32)
        off = SUBLANES - (CONV_K - 1)
        acc = xp_ref[b, off:off + L, :] * conv_ref[0:1, :]
        for j in range(1, CONV_K):
            acc = acc + xp_ref[b, off + j:off + j + L, :] * conv_ref[j:j + 1, :]
        xp_ref[b, 0:SUBLANES, :] = xp_ref[b, L:L + SUBLANES, :]
        qk.append(acc * _sigmoid(acc))
        g = gates_ref[b] + gbias_ref[...]
        gc.append(g)
        ball.append(jnp.dot(tril_f, _log_sigmoid(g), precision=lax.Precision.HIGHEST,
                            preferred_element_type=F32))
    for b in range(batch):
        gc_t.append(gc[b].T)
        ball_t.append(ball[b].T)

    st = []
    for b, h in chains:
        s = b * M_HEADS + h
        q = qk[b][:, h * DH:(h + 1) * DH]
        k = qk[b][:, D_MLSTM + h * DH:D_MLSTM + (h + 1) * DH] * (DH ** -0.5)
        d = dict(
            s=s, b=b, h=h, q=q, k=k, qb=q.astype(BF16), kb=k.astype(BF16),
            v=qkvo_ref[b, :, 2 * D_MLSTM + h * DH:2 * D_MLSTM + (h + 1) * DH],
            b_col=ball[b][:, M_HEADS + h:M_HEADS + h + 1],
            b_row=ball_t[b][M_HEADS + h:M_HEADS + h + 1, :],
            li_col=gc[b][:, h:h + 1],
            li_row=gc_t[b][h:h + 1, :],
            m_prev=m_ref[s][0:1, 0:1],
            ct_prev=ct_ref[s],
            n_prev=n_ref[s],
        )
        d["b_last"] = d["b_col"][L - 1:L, :]
        st.append(d)
    for d in st:
        d["qkt"] = lax.dot_general(d["qb"], d["kb"], (((1,), (1,)), ((), ())),
                                   preferred_element_type=F32)
        d["q_ct"] = jnp.dot(d["qb"], d["ct_prev"].astype(BF16), preferred_element_type=F32)
    for d in st:
        d["dmat"] = jnp.where(tril, d["b_col"] - d["b_row"] + d["li_row"], -jnp.inf)
        d["a_inter"] = d["b_col"] + d["m_prev"]
    for d in st:
        d["m_t"] = jnp.maximum(d["a_inter"], jnp.max(d["dmat"], axis=-1, keepdims=True))

    for d in st:
        d["w"] = jnp.exp(d["dmat"] - d["m_t"]) * d["qkt"]
        d["inter"] = jnp.exp(d["a_inter"] - d["m_t"])
    for d in st:
        d["num"] = (jnp.dot(d["w"].astype(BF16), d["v"], preferred_element_type=F32)
                    + d["inter"] * d["q_ct"])
        d["den"] = (jnp.sum(d["w"], axis=-1, keepdims=True)
                    + d["inter"] * jnp.sum(d["q"] * d["n_prev"], axis=-1, keepdims=True))
    for d in st:
        d["hh"] = d["num"] * (1.0 / jnp.maximum(jnp.abs(d["den"]), jnp.exp(-d["m_t"])))

    for d in st:
        d["mu"] = jnp.mean(d["hh"], axis=-1, keepdims=True)
    for d in st:
        d["xc"] = d["hh"] - d["mu"]
        d["var"] = jnp.mean(d["xc"] * d["xc"], axis=-1, keepdims=True)
    for d in st:
        b, h = d["b"], d["h"]
        y = d["xc"] * lax.rsqrt(d["var"] + EPS) * ngain_ref[:, h * DH:(h + 1) * DH]
        o = qkvo_ref[b, :, 3 * D_MLSTM + h * DH:3 * D_MLSTM + (h + 1) * DH].astype(F32)
        out_ref[b, :, h * DH:(h + 1) * DH] = (_sigmoid(o) * y).astype(out_ref.dtype)

    for d in st:
        d["g_col"] = d["b_last"] - d["b_col"] + d["li_col"]
        d["m_loc"] = jnp.max(d["g_col"], axis=0, keepdims=True)
    for d in st:
        d["w_col"] = jnp.exp(d["g_col"] - d["m_loc"])
        d["k_t"] = d["k"].T.astype(BF16)
    for d in st:
        wv = (d["w_col"] * d["v"].astype(F32)).astype(BF16)
        d["ct_loc"] = jnp.dot(d["k_t"], wv, preferred_element_type=F32)
        d["n_loc"] = jnp.sum(d["w_col"] * d["k"], axis=0, keepdims=True)
    for d in st:
        s = d["s"]
        m_new = jnp.maximum(d["b_last"] + d["m_prev"], d["m_loc"])
        a = jnp.exp(d["b_last"] + d["m_prev"] - m_new)
        cc = jnp.exp(d["m_loc"] - m_new)
        ct_ref[s] = a * d["ct_prev"] + cc * d["ct_loc"]
        n_ref[s] = a * d["n_prev"] + cc * d["n_loc"]
        m_ref[s] = jnp.broadcast_to(m_new, (SUBLANES, LANES))


def _mlstm(main3, gates3, conv_qk, gbias, ngain):
    B, S, _ = main3.shape
    nc = S // CHUNK
    return pl.pallas_call(
        functools.partial(_mlstm_kernel, batch=B),
        grid=(nc,),
        in_specs=[
            pl.BlockSpec((B, CHUNK, 4 * D_MLSTM), lambda c: (0, c, 0)),
            pl.BlockSpec((B, CHUNK, LANES), lambda c: (0, c, 0)),
            pl.BlockSpec((CONV_K, 2 * D_MLSTM), lambda c: (0, 0)),
            pl.BlockSpec((1, LANES), lambda c: (0, 0)),
            pl.BlockSpec((1, D_MLSTM), lambda c: (0, 0)),
        ],
        out_specs=pl.BlockSpec((B, CHUNK, D_MLSTM), lambda c: (0, c, 0)),
        out_shape=jax.ShapeDtypeStruct((B, S, D_MLSTM), BF16),
        scratch_shapes=[
            pltpu.VMEM((B, CHUNK + SUBLANES, 2 * D_MLSTM), F32),
            pltpu.VMEM((B * M_HEADS, M_HEAD_DIM, M_HEAD_DIM), F32),
            pltpu.VMEM((B * M_HEADS, 1, M_HEAD_DIM), F32),
            pltpu.VMEM((B * M_HEADS, SUBLANES, LANES), F32),
        ],
        compiler_params=_cparams(),
        name="mlstm",
    )(main3, gates3, conv_qk, gbias, ngain)


def _gmlp_kernel(uv_ref, lng_ref, ws_ref, bs_ref, og_ref, out_ref, prod_ref):
    L = CHUNK
    u = _gelu_tanh(uv_ref[:, 0:D_GMLP].astype(F32))
    vg = _gelu_tanh(uv_ref[:, D_GMLP:2 * D_GMLP].astype(F32))
    mu = jnp.mean(vg, axis=-1, keepdims=True)
    xc = vg - mu
    vg = xc * lax.rsqrt(jnp.mean(xc * xc, axis=-1, keepdims=True) + EPS) * lng_ref[...]
    vb = vg.astype(BF16)
    row = lax.broadcasted_iota(jnp.int32, (L, L), 0)
    col = lax.broadcasted_iota(jnp.int32, (L, L), 1)
    tril = row >= col
    for g in range(G_GROUPS):
        wg = jnp.where(tril, ws_ref[g], 0.0).astype(BF16)
        b_col = bs_ref[:, g:g + 1]
        for j in range(TG_GMLP // L):
            mixed = jnp.dot(wg, vb[j * L:(j + 1) * L, g * G_DIM:(g + 1) * G_DIM],
                            preferred_element_type=F32) + b_col
            prod_ref[j * L:(j + 1) * L, g * G_DIM:(g + 1) * G_DIM] = (
                u[j * L:(j + 1) * L, g * G_DIM:(g + 1) * G_DIM] * mixed)
    p = prod_ref[...]
    y = p * lax.rsqrt(jnp.mean(p * p, axis=-1, keepdims=True) + EPS) * og_ref[...]
    out_ref[...] = y.astype(out_ref.dtype)


def _gmlp(main, ln_gain, w_spatial, b_spatial_t, out_gain):
    T = main.shape[0]
    uv_block = 4 * D_MLSTM // (2 * D_GMLP)
    return pl.pallas_call(
        _gmlp_kernel,
        grid=(T // TG_GMLP,),
        in_specs=[
            pl.BlockSpec((TG_GMLP, 2 * D_GMLP), lambda i: (i, uv_block)),
            pl.BlockSpec((1, D_GMLP), lambda i: (0, 0)),
            pl.BlockSpec((G_GROUPS, CHUNK, CHUNK), lambda i: (0, 0, 0)),
            pl.BlockSpec((CHUNK, G_GROUPS), lambda i: (0, 0)),
            pl.BlockSpec((1, D_GMLP), lambda i: (0, 0)),
        ],
        out_specs=pl.BlockSpec((TG_GMLP, D_GMLP), lambda i: (i, 0)),
        out_shape=jax.ShapeDtypeStruct((T, D_GMLP), BF16),
        scratch_shapes=[pltpu.VMEM((TG_GMLP, D_GMLP), F32)],
        compiler_params=_cparams(),
        name="gmlp",
    )(main, ln_gain, w_spatial, b_spatial_t, out_gain)


def _outproj_kernel(hm_ref, hg_ref, x_ref, wo_ref, g2_ref, wr_ref, br_ref, tril_ref,
                    h_ref, hn_ref, route_ref, cnt_ref, carry_ref):
    i = pl.program_id(0)
    tm = TM_ROUTE

    @pl.when(i == 0)
    def _():
        carry_ref[...] = jnp.zeros_like(carry_ref)

    y = (jnp.dot(hm_ref[...], wo_ref[0:D_MLSTM, :], preferred_element_type=F32)
         + jnp.dot(hg_ref[...], wo_ref[D_MLSTM:D_MODEL, :], preferred_element_type=F32))
    h = x_ref[...] + y
    h_ref[...] = h
    hn = h * lax.rsqrt(jnp.mean(h * h, axis=-1, keepdims=True) + EPS) * g2_ref[...]
    _store_row_tiled(hn_ref, hn)

    logits = jnp.dot(hn.astype(BF16), wr_ref[...], preferred_element_type=F32) + br_ref[...]
    lane = lax.broadcasted_iota(jnp.int32, (tm, LANES), 1).astype(F32)
    l = jnp.where(lane < N_EXPERTS, logits, -jnp.inf)
    top_v, top_i = [], []
    for _ in range(TOP_K):
        mx = jnp.max(l, axis=-1, keepdims=True)
        ix = jnp.min(jnp.where(l == mx, lane, float(LANES)), axis=-1, keepdims=True)
        top_v.append(mx)
        top_i.append(ix)
        l = jnp.where(lane == ix, -jnp.inf, l)
    ex = [jnp.exp(v - top_v[0]) for v in top_v]
    inv = 1.0 / (ex[0] + ex[1] + ex[2] + ex[3])

    onehot = [lane == ix for ix in top_i]
    mask = (onehot[0].astype(F32) + onehot[1].astype(F32)
            + onehot[2].astype(F32) + onehot[3].astype(F32))
    incl = jnp.dot(tril_ref[...], mask.astype(BF16), preferred_element_type=F32)
    base = carry_ref[0:1, :] + incl - mask
    route = jnp.zeros((tm, LANES), F32)
    for k in range(TOP_K):
        rank = jnp.sum(jnp.where(onehot[k], base, 0.0), axis=-1, keepdims=True)
        route = jnp.where(lane == k, top_i[k], route)
        route = jnp.where(lane == TOP_K + k, rank, route)
        route = jnp.where(lane == 2 * TOP_K + k, ex[k] * inv, route)
    route_ref[...] = route
    carry_ref[...] = carry_ref[...] + incl[tm - 1:tm, :]
    cnt_ref[...] = carry_ref[...]


def _outproj(h_m, h_g, x2, w_out, gain2, w_router, b_router, tril):
    T = x2.shape[0]
    tm = TM_ROUTE
    return pl.pallas_call(
        _outproj_kernel,
        grid=(T // tm,),
        in_specs=[
            pl.BlockSpec((tm, D_MLSTM), lambda i: (i, 0)),
            pl.BlockSpec((tm, D_GMLP), lambda i: (i, 0)),
            pl.BlockSpec((tm, D_MODEL), lambda i: (i, 0)),
            pl.BlockSpec((D_MODEL, D_MODEL), lambda i: (0, 0)),
            pl.BlockSpec((1, D_MODEL), lambda i: (0, 0)),
            pl.BlockSpec((D_MODEL, LANES), lambda i: (0, 0)),
            pl.BlockSpec((1, LANES), lambda i: (0, 0)),
            pl.BlockSpec((tm, tm), lambda i: (0, 0)),
        ],
        out_specs=[
            pl.BlockSpec((tm, D_MODEL), lambda i: (i, 0)),
            pl.BlockSpec((tm * ROW_TILES, LANES), lambda i: (i, 0)),
            pl.BlockSpec((tm, LANES), lambda i: (i, 0)),
            pl.BlockSpec((SUBLANES, LANES), lambda i: (0, 0)),
        ],
        out_shape=[
            jax.ShapeDtypeStruct((T, D_MODEL), F32),
            jax.ShapeDtypeStruct((T * ROW_TILES, LANES), F32),
            jax.ShapeDtypeStruct((T, LANES), F32),
            jax.ShapeDtypeStruct((SUBLANES, LANES), F32),
        ],
        scratch_shapes=[pltpu.VMEM((SUBLANES, LANES), F32)],
        compiler_params=_cparams(),
        name="outproj_router",
    )(h_m, h_g, x2, w_out, gain2, w_router, b_router, tril)


def _dest_kernel(route_ref, pstart_ref, dest_ref):
    r = route_ref[...]
    tm = r.shape[0]
    lane = lax.broadcasted_iota(jnp.int32, (tm, LANES), 1).astype(F32)
    ps = pstart_ref[...]
    out = jnp.zeros((tm, LANES), F32)
    for k in range(TOP_K):
        ix = r[:, k:k + 1]
        rank = r[:, TOP_K + k:TOP_K + k + 1]
        d = rank + jnp.sum(jnp.where(lane == ix, ps, 0.0), axis=-1, keepdims=True)
        out = jnp.where(lane == k, d, out)
    dest_ref[...] = out.astype(jnp.int32)


def _dest(route, pstart):
    T = route.shape[0]
    tm = TM_ROUTE
    return pl.pallas_call(
        _dest_kernel,
        grid=(T // tm,),
        in_specs=[
            pl.BlockSpec((tm, LANES), lambda i: (i, 0)),
            pl.BlockSpec((1, LANES), lambda i: (0, 0)),
        ],
        out_specs=pl.BlockSpec((tm, LANES), lambda i: (i, 0)),
        out_shape=jax.ShapeDtypeStruct((T, LANES), jnp.int32),
        compiler_params=_cparams(),
        name="dest",
    )(route, pstart)


def _row_slice(n):
    return pl.ds(pl.multiple_of(n * ROW_TILES, ROW_TILES), ROW_TILES)


def _dispatch_kernel(dest_ref, pend_ref, hn_ref, xs_hbm, zero_ref, sem, zsem):
    i = pl.program_id(0)
    base = i * TD_DISPATCH
    blk_rows = BM_FFN * ROW_TILES

    @pl.when(i == 0)
    def _():
        zero_ref[...] = jnp.zeros_like(zero_ref)

        def zero_copy(e):
            start = pl.multiple_of((pend_ref[e] - BM_FFN) * ROW_TILES, blk_rows)
            return pltpu.make_async_copy(zero_ref, xs_hbm.at[pl.ds(start, blk_rows)], zsem)

        def nonempty(e):
            return pend_ref[e] > (pend_ref[e - 1] if e > 0 else 0)

        nb = xs_hbm.shape[0] // blk_rows

        def tail_copy(j):
            return pltpu.make_async_copy(
                zero_ref, xs_hbm.at[pl.ds((nb - 1 - j) * blk_rows, blk_rows)], zsem)

        def unused(j):
            return (nb - 1 - j) * BM_FFN >= pend_ref[N_EXPERTS - 1]

        for e in range(N_EXPERTS):
            @pl.when(nonempty(e))
            def _():
                zero_copy(e).start()

            @pl.when(unused(e))
            def _():
                tail_copy(e).start()
        for e in range(N_EXPERTS):
            @pl.when(nonempty(e))
            def _():
                zero_copy(e).wait()

            @pl.when(unused(e))
            def _():
                tail_copy(e).wait()

    def row_copy(t, k):
        d = dest_ref[(base + t) * TOP_K + k]
        return pltpu.make_async_copy(hn_ref.at[_row_slice(t)], xs_hbm.at[_row_slice(d)], sem)

    def issue(t, carry):
        for k in range(TOP_K):
            row_copy(t, k).start(priority=k % DMA_QUEUES)
        return carry

    def drain(t, carry):
        for k in range(TOP_K):
            row_copy(t, k).wait()
        return carry

    lax.fori_loop(0, TD_DISPATCH, issue, 0, unroll=8)
    lax.fori_loop(0, TD_DISPATCH, drain, 0, unroll=8)


def _dispatch(dest_flat, pend, hn_rt, nbuf):
    T = hn_rt.shape[0] // ROW_TILES
    return pl.pallas_call(
        _dispatch_kernel,
        grid_spec=pltpu.PrefetchScalarGridSpec(
            num_scalar_prefetch=2,
            grid=(T // TD_DISPATCH,),
            in_specs=[pl.BlockSpec((TD_DISPATCH * ROW_TILES, LANES), lambda i, d, p: (i, 0))],
            out_specs=pl.BlockSpec(memory_space=pl.ANY),
            scratch_shapes=[
                pltpu.VMEM((BM_FFN * ROW_TILES, LANES), F32),
                pltpu.SemaphoreType.DMA,
                pltpu.SemaphoreType.DMA,
            ],
        ),
        out_shape=jax.ShapeDtypeStruct((nbuf * ROW_TILES, LANES), F32),
        compiler_params=_cparams(),
        name="dispatch",
    )(dest_flat, pend, hn_rt)


def _ffn_kernel(bstart_ref, bcnt_ref, xs_hbm, wup_ref, bup_ref, wdn_ref, bdn_ref, y_hbm,
                xbuf, ybuf, wup_bf, wdn_bf, in_sem, out_sem):
    e = pl.program_id(0)
    blk_rows = BM_FFN * ROW_TILES
    b0 = bstart_ref[e]
    cnt = bcnt_ref[e]

    def block_rows(j):
        return pl.ds(pl.multiple_of(j * blk_rows, blk_rows), blk_rows)

    def in_copy(j, slot):
        return pltpu.make_async_copy(xs_hbm.at[block_rows(j)], xbuf.at[slot], in_sem.at[slot])

    def out_copy(j, slot):
        return pltpu.make_async_copy(ybuf.at[slot], y_hbm.at[block_rows(j)], out_sem.at[slot])

    @pl.when(cnt > 0)
    def _():
        in_copy(b0, 0).start(priority=1)
        wup_bf[...] = wup_ref[0].astype(BF16)
        wdn_bf[...] = wdn_ref[0].astype(BF16)

        def body(j, carry):
            slot = j % 2
            in_copy(b0 + j, slot).wait()

            @pl.when(j + 1 < cnt)
            def _():
                in_copy(b0 + j + 1, 1 - slot).start(priority=1)

            @pl.when(j >= 2)
            def _():
                out_copy(b0 + j - 2, slot).wait()

            xb = _load_row_tiled(xbuf.at[slot], BM_FFN).astype(BF16)
            hb = jnp.dot(xb, wup_bf[...], preferred_element_type=F32) + bup_ref[0]
            glu = jnp.minimum(hb[:, 0:D_FF], SWIGLU_LIMIT)
            lin = jnp.clip(hb[:, D_FF:2 * D_FF], -SWIGLU_LIMIT, SWIGLU_LIMIT)
            act = glu * _sigmoid(SWIGLU_ALPHA * glu) * (lin + 1.0)
            y = jnp.dot(act.astype(BF16), wdn_bf[...], preferred_element_type=F32) + bdn_ref[0]
            _store_row_tiled(ybuf.at[slot], y)
            out_copy(b0 + j, slot).start(priority=1)
            return carry

        lax.fori_loop(0, cnt, body, 0)

        @pl.when(cnt >= 2)
        def _():
            out_copy(b0 + cnt - 2, cnt % 2).wait()

        out_copy(b0 + cnt - 1, (cnt - 1) % 2).wait()

    @pl.when(e == N_EXPERTS - 1)
    def _():
        nb = y_hbm.shape[0] // blk_rows
        n_valid = b0 + cnt
        ybuf[0] = jnp.zeros((blk_rows, LANES), F32)
        for j in range(N_EXPERTS):
            @pl.when(nb - 1 - j >= n_valid)
            def _():
                out_copy(nb - 1 - j, 0).start()
        for j in range(N_EXPERTS):
            @pl.when(nb - 1 - j >= n_valid)
            def _():
                out_copy(nb - 1 - j, 0).wait()


def _ffn(bstart, bcnt, xs_rt, w_up, b_up3, w_down, b_down3):
    blk_rows = BM_FFN * ROW_TILES
    wspec = lambda shape: pl.BlockSpec(shape, lambda e, bs, bc: (e, 0, 0))
    return pl.pallas_call(
        _ffn_kernel,
        grid_spec=pltpu.PrefetchScalarGridSpec(
            num_scalar_prefetch=2,
            grid=(N_EXPERTS,),
            in_specs=[
                pl.BlockSpec(memory_space=pl.ANY),
                wspec((1, D_MODEL, 2 * D_FF)),
                wspec((1, 1, 2 * D_FF)),
                wspec((1, D_FF, D_MODEL)),
                wspec((1, 1, D_MODEL)),
            ],
            out_specs=pl.BlockSpec(memory_space=pl.ANY),
            scratch_shapes=[
                pltpu.VMEM((2, blk_rows, LANES), F32),
                pltpu.VMEM((2, blk_rows, LANES), F32),
                pltpu.VMEM((D_MODEL, 2 * D_FF), BF16),
                pltpu.VMEM((D_FF, D_MODEL), BF16),
                pltpu.SemaphoreType.DMA((2,)),
                pltpu.SemaphoreType.DMA((2,)),
            ],
        ),
        out_shape=jax.ShapeDtypeStruct(xs_rt.shape, F32),
        compiler_params=_cparams(),
        name="expert_ffn",
    )(bstart, bcnt, xs_rt, w_up, b_up3, w_down, b_down3)


def _combine_kernel(dest_ref, route_ref, h_ref, gain_ref, yb_hbm, out_ref, buf_ref, sem):
    tm = TM_COMBINE
    base = pl.program_id(0) * tm

    def row_copy(t, k):
        d = dest_ref[(base + t) * TOP_K + k]
        return pltpu.make_async_copy(yb_hbm.at[_row_slice(d)], buf_ref.at[_row_slice(k * tm + t)],
                                     sem)

    def issue(t, carry):
        for k in range(TOP_K):
            row_copy(t, k).start(priority=k % DMA_QUEUES)
        return carry

    def drain(t, carry):
        for k in range(TOP_K):
            row_copy(t, k).wait()
        return carry

    lax.fori_loop(0, tm, issue, 0, unroll=8)
    lax.fori_loop(0, tm, drain, 0, unroll=8)

    r = route_ref[...]
    acc = h_ref[...]
    for k in range(TOP_K):
        acc = acc + r[:, 2 * TOP_K + k:2 * TOP_K + k + 1] * _load_row_tiled(buf_ref, tm, k * tm)
    out = acc * lax.rsqrt(jnp.mean(acc * acc, axis=-1, keepdims=True) + EPS) * gain_ref[...]
    out_ref[...] = out


def _combine(dest_flat, route, h, gain, yb_rt):
    T = h.shape[0]
    tm = TM_COMBINE
    return pl.pallas_call(
        _combine_kernel,
        grid_spec=pltpu.PrefetchScalarGridSpec(
            num_scalar_prefetch=1,
            grid=(T // tm,),
            in_specs=[
                pl.BlockSpec((tm, LANES), lambda i, d: (i, 0)),
                pl.BlockSpec((tm, D_MODEL), lambda i, d: (i, 0)),
                pl.BlockSpec((1, D_MODEL), lambda i, d: (0, 0)),
                pl.BlockSpec(memory_space=pl.ANY),
            ],
            out_specs=pl.BlockSpec((tm, D_MODEL), lambda i, d: (i, 0)),
            scratch_shapes=[
                pltpu.VMEM((TOP_K * tm * ROW_TILES, LANES), F32),
                pltpu.SemaphoreType.DMA,
            ],
        ),
        out_shape=jax.ShapeDtypeStruct((T, D_MODEL), F32),
        compiler_params=_cparams(),
        name="combine",
    )(dest_flat, route, h, gain, yb_rt)


def _layer(x2, B, S, norm1_gain, w_in, conv_qk, b_igate, b_fgate, mlstm_norm_gain, gmlp_ln_gain,
           w_spatial, b_spatial, gmlp_out_gain, w_out, norm2_gain, w_router, b_router,
           w_up, b_up, w_down, b_down):
    T = B * S
    n_gate = 2 * M_HEADS
    g0 = 4 * D_MLSTM
    w_main = jnp.concatenate([w_in[:, :g0], w_in[:, g0 + n_gate:]], axis=1).astype(BF16)
    w_gate = jnp.pad(w_in[:, g0:g0 + n_gate], ((0, 0), (0, LANES - n_gate))).astype(BF16)
    gbias = jnp.pad(jnp.concatenate([b_igate, b_fgate]), (0, LANES - n_gate)).reshape(1, LANES)

    main, gates = _inproj(x2, norm1_gain.reshape(1, D_MODEL), w_main, w_gate)
    h_m = _mlstm(main.reshape(B, S, D_MAIN), gates.reshape(B, S, LANES), conv_qk, gbias,
                 mlstm_norm_gain.reshape(1, D_MLSTM)).reshape(T, D_MLSTM)
    h_g = _gmlp(main, gmlp_ln_gain.reshape(1, D_GMLP), w_spatial, b_spatial.T,
                gmlp_out_gain.reshape(1, D_GMLP))

    w_r = jnp.pad(w_router, ((0, 0), (0, LANES - N_EXPERTS))).astype(BF16)
    b_r = jnp.pad(b_router, (0, LANES - N_EXPERTS)).reshape(1, LANES)
    tril = jnp.tril(jnp.ones((TM_ROUTE, TM_ROUTE), BF16))
    h, hn_rt, route, cnt = _outproj(h_m, h_g, x2, w_out.astype(BF16),
                                    norm2_gain.reshape(1, D_MODEL), w_r, b_r, tril)

    counts = cnt[0, :N_EXPERTS].astype(jnp.int32)
    padded = (counts + BM_FFN - 1) // BM_FFN * BM_FFN
    pend = jnp.cumsum(padded)
    pstart = pend - padded
    nb = (T * TOP_K) // BM_FFN + N_EXPERTS
    pstart_row = jnp.pad(pstart.astype(F32), (0, LANES - N_EXPERTS)).reshape(1, LANES)

    dest = _dest(route, pstart_row)
    dest_flat = dest[:, :TOP_K].reshape(T * TOP_K)

    xs_rt = _dispatch(dest_flat, pend.astype(jnp.int32), hn_rt, nb * BM_FFN)
    yb_rt = _ffn((pstart // BM_FFN).astype(jnp.int32), (padded // BM_FFN).astype(jnp.int32), xs_rt,
                 w_up, b_up.reshape(N_EXPERTS, 1, 2 * D_FF), w_down,
                 b_down.reshape(N_EXPERTS, 1, D_MODEL))
    return dest_flat, route, h, yb_rt


def kernel(x, norm1_gain, w_in, conv_qk, b_igate, b_fgate, mlstm_norm_gain, gmlp_ln_gain,
           w_spatial, b_spatial, gmlp_out_gain, w_out, norm2_gain, w_router, b_router,
           w_up, b_up, w_down, b_down, final_gain):
    B, S, D = x.shape
    depth = norm1_gain.shape[0]
    assert depth == 1 and D == D_MODEL and S % CHUNK == 0
    x2 = x.reshape(B * S, D)
    l = 0
    dest_flat, route, h, yb_rt = _layer(
        x2, B, S, norm1_gain[l], w_in[l], conv_qk[l], b_igate[l], b_fgate[l], mlstm_norm_gain[l],
        gmlp_ln_gain[l], w_spatial[l], b_spatial[l], gmlp_out_gain[l], w_out[l], norm2_gain[l],
        w_router[l], b_router[l], w_up[l], b_up[l], w_down[l], b_down[l])
    out = _combine(dest_flat, route, h, final_gain.reshape(1, D_MODEL), yb_rt)
    return out.reshape(B, S, D)
```

```python
import functools

import jax
import jax.numpy as jnp
from jax import lax
from jax.experimental import pallas as pl
from jax.experimental.pallas import tpu as pltpu

F32 = jnp.float32
BF16 = jnp.bfloat16

D_MODEL = 1024
M_HEADS = 4
M_HEAD_DIM = 128
D_MLSTM = M_HEADS * M_HEAD_DIM
D_GMLP = D_MODEL - D_MLSTM
G_GROUPS = 4
G_DIM = D_GMLP // G_GROUPS
CHUNK = 128
CONV_K = 4
N_EXPERTS = 32
TOP_K = 4
D_FF = D_MODEL
SWIGLU_LIMIT = 7.0
SWIGLU_ALPHA = 1.702
EPS = 1e-6

LANES = 128
SUBLANES = 8
ROW_TILES = D_MODEL // LANES
assert ROW_TILES == SUBLANES
D_MAIN = 4 * D_MLSTM + 2 * D_GMLP
TM_PROJ = 512
TG_GMLP = 512
TM_ROUTE = 512
TD_DISPATCH = 512
TM_COMBINE = 512
BM_FFN = 512
DMA_QUEUES = 2
VMEM_LIMIT = 56 * 1024 * 1024


def _cparams(n_axes=1):
    return pltpu.CompilerParams(
        dimension_semantics=("arbitrary",) * n_axes, vmem_limit_bytes=VMEM_LIMIT)


def _sigmoid(x):
    return 1.0 / (1.0 + jnp.exp(-x))


def _gelu_tanh(x):
    c = 0.7978845608028654
    return x * (0.5 * (1.0 + jnp.tanh(c * (x + 0.044715 * (x * x * x)))))


def _log_sigmoid(x):
    return jnp.minimum(x, 0.0) - jnp.log1p(jnp.exp(-jnp.abs(x)))


def _store_row_tiled(ref, x, row0=0):
    n_rows = x.shape[0]
    for g in range(n_rows // SUBLANES):
        for j in range(ROW_TILES):
            ref[pl.ds((row0 + g * SUBLANES) * ROW_TILES + j, SUBLANES, stride=ROW_TILES), :] = (
                x[g * SUBLANES:(g + 1) * SUBLANES, j * LANES:(j + 1) * LANES])


def _load_row_tiled(ref, n_rows, row0=0):
    groups = []
    for g in range(n_rows // SUBLANES):
        tiles = [ref[pl.ds((row0 + g * SUBLANES) * ROW_TILES + j, SUBLANES, stride=ROW_TILES), :]
                 for j in range(ROW_TILES)]
        groups.append(jnp.concatenate(tiles, axis=1))
    return jnp.concatenate(groups, axis=0)


def _inproj_kernel(x_ref, gain_ref, wm_ref, wg_ref, main_ref, gate_ref):
    x = x_ref[...]
    xn = x * lax.rsqrt(jnp.mean(x * x, axis=-1, keepdims=True) + EPS) * gain_ref[...]
    xb = xn.astype(BF16)
    main_ref[...] = jnp.dot(xb, wm_ref[...], preferred_element_type=F32).astype(BF16)
    gate_ref[...] = jnp.dot(xb, wg_ref[...], preferred_element_type=F32)


def _inproj(x2, gain, w_main, w_gate):
    T = x2.shape[0]
    return pl.pallas_call(
        _inproj_kernel,
        grid=(T // TM_PROJ,),
        in_specs=[
            pl.BlockSpec((TM_PROJ, D_MODEL), lambda i: (i, 0)),
            pl.BlockSpec((1, D_MODEL), lambda i: (0, 0)),
            pl.BlockSpec((D_MODEL, D_MAIN), lambda i: (0, 0)),
            pl.BlockSpec((D_MODEL, LANES), lambda i: (0, 0)),
        ],
        out_specs=[
            pl.BlockSpec((TM_PROJ, D_MAIN), lambda i: (i, 0)),
            pl.BlockSpec((TM_PROJ, LANES), lambda i: (i, 0)),
        ],
        out_shape=[
            jax.ShapeDtypeStruct((T, D_MAIN), BF16),
            jax.ShapeDtypeStruct((T, LANES), F32),
        ],
        compiler_params=_cparams(),
        name="inproj",
    )(x2, gain, w_main, w_gate)


def _mlstm_kernel(qkvo_ref, gates_ref, conv_ref, gbias_ref, ngain_ref, out_ref,
                  xp_ref, ct_ref, n_ref, m_ref, *, batch):
    c = pl.program_id(0)
    L, DH = CHUNK, M_HEAD_DIM

    @pl.when(c == 0)
    def _():
        xp_ref[:, 0:SUBLANES, :] = jnp.zeros((batch, SUBLANES, 2 * D_MLSTM), F32)
        ct_ref[...] = jnp.zeros_like(ct_ref)
        n_ref[...] = jnp.zeros_like(n_ref)
        m_ref[...] = jnp.zeros_like(m_ref)

    row = lax.broadcasted_iota(jnp.int32, (L, L), 0)
    col = lax.broadcasted_iota(jnp.int32, (L, L), 1)
    tril = row >= col
    tril_f = tril.astype(F32)
    chains = [(b, h) for b in range(batch) for h in range(M_HEADS)]


    qk, gc, ball, gc_t, ball_t = [], [], [], [], []
    for b in range(batch):
        xp_ref[b, SUBLANES:SUBLANES + L, :] = qkvo_ref[b, :, 0:2 * D_MLSTM].astype(F32)
        off = SUBLANES - (CONV_K - 1)
        acc = xp_ref[b, off:off + L, :] * conv_ref[0:1, :]
        for j in range(1, CONV_K):
            acc = acc + xp_ref[b, off + j:off + j + L, :] * conv_ref[j:j + 1, :]
        xp_ref[b, 0:SUBLANES, :] = xp_ref[b, L:L + SUBLANES, :]
        qk.append(acc * _sigmoid(acc))
        g = gates_ref[b] + gbias_ref[...]
        gc.append(g)
        ball.append(jnp.dot(tril_f, _log_sigmoid(g), precision=lax.Precision.HIGHEST,
                            preferred_element_type=F32))
    for b in range(batch):
        gc_t.append(gc[b].T)
        ball_t.append(ball[b].T)

    st = []
    for b, h in chains:
        s = b * M_HEADS + h
        q = qk[b][:, h * DH:(h + 1) * DH]
        k = qk[b][:, D_MLSTM + h * DH:D_MLSTM + (h + 1) * DH] * (DH ** -0.5)
        d = dict(
            s=s, b=b, h=h, q=q, k=k, qb=q.astype(BF16), kb=k.astype(BF16),
            v=qkvo_ref[b, :, 2 * D_MLSTM + h * DH:2 * D_MLSTM + (h + 1) * DH],
            b_col=ball[b][:, M_HEADS + h:M_HEADS + h + 1],
            b_row=ball_t[b][M_HEADS + h:M_HEADS + h + 1, :],
            li_col=gc[b][:, h:h + 1],
            li_row=gc_t[b][h:h + 1, :],
            m_prev=m_ref[s][0:1, 0:1],
            ct_prev=ct_ref[s],
            n_prev=n_ref[s],
        )
        d["b_last"] = d["b_col"][L - 1:L, :]
        st.append(d)
    for d in st:
        d["qkt"] = lax.dot_general(d["qb"], d["kb"], (((1,), (1,)), ((), ())),
                                   preferred_element_type=F32)
        d["q_ct"] = jnp.dot(d["qb"], d["ct_prev"].astype(BF16), preferred_element_type=F32)
    for d in st:
        d["dmat"] = jnp.where(tril, d["b_col"] - d["b_row"] + d["li_row"], -jnp.inf)
        d["a_inter"] = d["b_col"] + d["m_prev"]
    for d in st:
        d["m_t"] = jnp.maximum(d["a_inter"], jnp.max(d["dmat"], axis=-1, keepdims=True))

    for d in st:
        d["w"] = jnp.exp(d["dmat"] - d["m_t"]) * d["qkt"]
        d["inter"] = jnp.exp(d["a_inter"] - d["m_t"])
    for d in st:
        d["num"] = (jnp.dot(d["w"].astype(BF16), d["v"], preferred_element_type=F32)
                    + d["inter"] * d["q_ct"])
        d["den"] = (jnp.sum(d["w"], axis=-1, keepdims=True)
                    + d["inter"] * jnp.sum(d["q"] * d["n_prev"], axis=-1, keepdims=True))
    for d in st:
        d["hh"] = d["num"] * (1.0 / jnp.maximum(jnp.abs(d["den"]), jnp.exp(-d["m_t"])))

    for d in st:
        d["mu"] = jnp.mean(d["hh"], axis=-1, keepdims=True)
    for d in st:
        d["xc"] = d["hh"] - d["mu"]
        d["var"] = jnp.mean(d["xc"] * d["xc"], axis=-1, keepdims=True)
    for d in st:
        b, h = d["b"], d["h"]
        y = d["xc"] * lax.rsqrt(d["var"] + EPS) * ngain_ref[:, h * DH:(h + 1) * DH]
        o = qkvo_ref[b, :, 3 * D_MLSTM + h * DH:3 * D_MLSTM + (h + 1) * DH].astype(F32)
        out_ref[b, :, h * DH:(h + 1) * DH] = (_sigmoid(o) * y).astype(out_ref.dtype)

    for d in st:
        d["g_col"] = d["b_last"] - d["b_col"] + d["li_col"]
        d["m_loc"] = jnp.max(d["g_col"], axis=0, keepdims=True)
    for d in st:
        d["w_col"] = jnp.exp(d["g_col"] - d["m_loc"])
        d["k_t"] = d["k"].T.astype(BF16)
    for d in st:
        wv = (d["w_col"] * d["v"].astype(F32)).astype(BF16)
        d["ct_loc"] = jnp.dot(d["k_t"], wv, preferred_element_type=F32)
        d["n_loc"] = jnp.sum(d["w_col"] * d["k"], axis=0, keepdims=True)
    for d in st:
        s = d["s"]
        m_new = jnp.maximum(d["b_last"] + d["m_prev"], d["m_loc"])
        a = jnp.exp(d["b_last"] + d["m_prev"] - m_new)
        cc = jnp.exp(d["m_loc"] - m_new)
        ct_ref[s] = a * d["ct_prev"] + cc * d["ct_loc"]
        n_ref[s] = a * d["n_prev"] + cc * d["n_loc"]
        m_ref[s] = jnp.broadcast_to(m_new, (SUBLANES, LANES))


def _mlstm(main3, gates3, conv_qk, gbias, ngain):
    B, S, _ = main3.shape
    nc = S // CHUNK
    return pl.pallas_call(
        functools.partial(_mlstm_kernel, batch=B),
        grid=(nc,),
        in_specs=[
            pl.BlockSpec((B, CHUNK, 4 * D_MLSTM), lambda c: (0, c, 0)),
            pl.BlockSpec((B, CHUNK, LANES), lambda c: (0, c, 0)),
            pl.BlockSpec((CONV_K, 2 * D_MLSTM), lambda c: (0, 0)),
            pl.BlockSpec((1, LANES), lambda c: (0, 0)),
            pl.BlockSpec((1, D_MLSTM), lambda c: (0, 0)),
        ],
        out_specs=pl.BlockSpec((B, CHUNK, D_MLSTM), lambda c: (0, c, 0)),
        out_shape=jax.ShapeDtypeStruct((B, S, D_MLSTM), BF16),
        scratch_shapes=[
            pltpu.VMEM((B, CHUNK + SUBLANES, 2 * D_MLSTM), F32),
            pltpu.VMEM((B * M_HEADS, M_HEAD_DIM, M_HEAD_DIM), F32),
            pltpu.VMEM((B * M_HEADS, 1, M_HEAD_DIM), F32),
            pltpu.VMEM((B * M_HEADS, SUBLANES, LANES), F32),
        ],
        compiler_params=_cparams(),
        name="mlstm",
    )(main3, gates3, conv_qk, gbias, ngain)


def _gmlp_kernel(uv_ref, lng_ref, ws_ref, bs_ref, og_ref, out_ref, prod_ref):
    L = CHUNK
    u = _gelu_tanh(uv_ref[:, 0:D_GMLP].astype(F32))
    vg = _gelu_tanh(uv_ref[:, D_GMLP:2 * D_GMLP].astype(F32))
    mu = jnp.mean(vg, axis=-1, keepdims=True)
    xc = vg - mu
    vg = xc * lax.rsqrt(jnp.mean(xc * xc, axis=-1, keepdims=True) + EPS) * lng_ref[...]
    vb = vg.astype(BF16)
    row = lax.broadcasted_iota(jnp.int32, (L, L), 0)
    col = lax.broadcasted_iota(jnp.int32, (L, L), 1)
    tril = row >= col
    for g in range(G_GROUPS):
        wg = jnp.where(tril, ws_ref[g], 0.0).astype(BF16)
        b_col = bs_ref[:, g:g + 1]
        for j in range(TG_GMLP // L):
            mixed = jnp.dot(wg, vb[j * L:(j + 1) * L, g * G_DIM:(g + 1) * G_DIM],
                            preferred_element_type=F32) + b_col
            prod_ref[j * L:(j + 1) * L, g * G_DIM:(g + 1) * G_DIM] = (
                u[j * L:(j + 1) * L, g * G_DIM:(g + 1) * G_DIM] * mixed)
    p = prod_ref[...]
    y = p * lax.rsqrt(jnp.mean(p * p, axis=-1, keepdims=True) + EPS) * og_ref[...]
    out_ref[...] = y.astype(out_ref.dtype)


def _gmlp(main, ln_gain, w_spatial, b_spatial_t, out_gain):
    T = main.shape[0]
    uv_block = 4 * D_MLSTM // (2 * D_GMLP)
    return pl.pallas_call(
        _gmlp_kernel,
        grid=(T // TG_GMLP,),
        in_specs=[
            pl.BlockSpec((TG_GMLP, 2 * D_GMLP), lambda i: (i, uv_block)),
            pl.BlockSpec((1, D_GMLP), lambda i: (0, 0)),
            pl.BlockSpec((G_GROUPS, CHUNK, CHUNK), lambda i: (0, 0, 0)),
            pl.BlockSpec((CHUNK, G_GROUPS), lambda i: (0, 0)),
            pl.BlockSpec((1, D_GMLP), lambda i: (0, 0)),
        ],
        out_specs=pl.BlockSpec((TG_GMLP, D_GMLP), lambda i: (i, 0)),
        out_shape=jax.ShapeDtypeStruct((T, D_GMLP), BF16),
        scratch_shapes=[pltpu.VMEM((TG_GMLP, D_GMLP), F32)],
        compiler_params=_cparams(),
        name="gmlp",
    )(main, ln_gain, w_spatial, b_spatial_t, out_gain)


def _outproj_kernel(hm_ref, hg_ref, x_ref, wo_ref, g2_ref, wr_ref, br_ref, tril_ref,
                    h_ref, hn_ref, route_ref, cnt_ref, carry_ref):
    i = pl.program_id(0)
    tm = TM_ROUTE

    @pl.when(i == 0)
    def _():
        carry_ref[...] = jnp.zeros_like(carry_ref)

    y = (jnp.dot(hm_ref[...], wo_ref[0:D_MLSTM, :], preferred_element_type=F32)
         + jnp.dot(hg_ref[...], wo_ref[D_MLSTM:D_MODEL, :], preferred_element_type=F32))
    h = x_ref[...] + y
    h_ref[...] = h
    hn = h * lax.rsqrt(jnp.mean(h * h, axis=-1, keepdims=True) + EPS) * g2_ref[...]
    _store_row_tiled(hn_ref, hn)

    logits = jnp.dot(hn.astype(BF16), wr_ref[...], preferred_element_type=F32) + br_ref[...]
    lane = lax.broadcasted_iota(jnp.int32, (tm, LANES), 1).astype(F32)
    l = jnp.where(lane < N_EXPERTS, logits, -jnp.inf)
    top_v, top_i = [], []
    for _ in range(TOP_K):
        mx = jnp.max(l, axis=-1, keepdims=True)
        ix = jnp.min(jnp.where(l == mx, lane, float(LANES)), axis=-1, keepdims=True)
        top_v.append(mx)
        top_i.append(ix)
        l = jnp.where(lane == ix, -jnp.inf, l)
    ex = [jnp.exp(v - top_v[0]) for v in top_v]
    inv = 1.0 / (ex[0] + ex[1] + ex[2] + ex[3])

    onehot = [lane == ix for ix in top_i]
    mask = (onehot[0].astype(F32) + onehot[1].astype(F32)
            + onehot[2].astype(F32) + onehot[3].astype(F32))
    incl = jnp.dot(tril_ref[...], mask.astype(BF16), preferred_element_type=F32)
    base = carry_ref[0:1, :] + incl - mask
    route = jnp.zeros((tm, LANES), F32)
    for k in range(TOP_K):
        rank = jnp.sum(jnp.where(onehot[k], base, 0.0), axis=-1, keepdims=True)
        route = jnp.where(lane == k, top_i[k], route)
        route = jnp.where(lane == TOP_K + k, rank, route)
        route = jnp.where(lane == 2 * TOP_K + k, ex[k] * inv, route)
    route_ref[...] = route
    carry_ref[...] = carry_ref[...] + incl[tm - 1:tm, :]
    cnt_ref[...] = carry_ref[...]


def _outproj(h_m, h_g, x2, w_out, gain2, w_router, b_router, tril):
    T = x2.shape[0]
    tm = TM_ROUTE
    return pl.pallas_call(
        _outproj_kernel,
        grid=(T // tm,),
        in_specs=[
            pl.BlockSpec((tm, D_MLSTM), lambda i: (i, 0)),
            pl.BlockSpec((tm, D_GMLP), lambda i: (i, 0)),
            pl.BlockSpec((tm, D_MODEL), lambda i: (i, 0)),
            pl.BlockSpec((D_MODEL, D_MODEL), lambda i: (0, 0)),
            pl.BlockSpec((1, D_MODEL), lambda i: (0, 0)),
            pl.BlockSpec((D_MODEL, LANES), lambda i: (0, 0)),
            pl.BlockSpec((1, LANES), lambda i: (0, 0)),
            pl.BlockSpec((tm, tm), lambda i: (0, 0)),
        ],
        out_specs=[
            pl.BlockSpec((tm, D_MODEL), lambda i: (i, 0)),
            pl.BlockSpec((tm * ROW_TILES, LANES), lambda i: (i, 0)),
            pl.BlockSpec((tm, LANES), lambda i: (i, 0)),
            pl.BlockSpec((SUBLANES, LANES), lambda i: (0, 0)),
        ],
        out_shape=[
            jax.ShapeDtypeStruct((T, D_MODEL), F32),
            jax.ShapeDtypeStruct((T * ROW_TILES, LANES), F32),
            jax.ShapeDtypeStruct((T, LANES), F32),
            jax.ShapeDtypeStruct((SUBLANES, LANES), F32),
        ],
        scratch_shapes=[pltpu.VMEM((SUBLANES, LANES), F32)],
        compiler_params=_cparams(),
        name="outproj_router",
    )(h_m, h_g, x2, w_out, gain2, w_router, b_router, tril)


def _dest_kernel(route_ref, pstart_ref, dest_ref):
    r = route_ref[...]
    tm = r.shape[0]
    lane = lax.broadcasted_iota(jnp.int32, (tm, LANES), 1).astype(F32)
    ps = pstart_ref[...]
    out = jnp.zeros((tm, LANES), F32)
    for k in range(TOP_K):
        ix = r[:, k:k + 1]
        rank = r[:, TOP_K + k:TOP_K + k + 1]
        d = rank + jnp.sum(jnp.where(lane == ix, ps, 0.0), axis=-1, keepdims=True)
        out = jnp.where(lane == k, d, out)
    dest_ref[...] = out.astype(jnp.int32)


def _dest(route, pstart):
    T = route.shape[0]
    tm = TM_ROUTE
    return pl.pallas_call(
        _dest_kernel,
        grid=(T // tm,),
        in_specs=[
            pl.BlockSpec((tm, LANES), lambda i: (i, 0)),
            pl.BlockSpec((1, LANES), lambda i: (0, 0)),
        ],
        out_specs=pl.BlockSpec((tm, LANES), lambda i: (i, 0)),
        out_shape=jax.ShapeDtypeStruct((T, LANES), jnp.int32),
        compiler_params=_cparams(),
        name="dest",
    )(route, pstart)


def _row_slice(n):
    return pl.ds(pl.multiple_of(n * ROW_TILES, ROW_TILES), ROW_TILES)


def _dispatch_kernel(dest_ref, pend_ref, hn_ref, xs_hbm, zero_ref, sem, zsem):
    i = pl.program_id(0)
    base = i * TD_DISPATCH
    blk_rows = BM_FFN * ROW_TILES

    @pl.when(i == 0)
    def _():
        zero_ref[...] = jnp.zeros_like(zero_ref)

        def zero_copy(e):
            start = pl.multiple_of((pend_ref[e] - BM_FFN) * ROW_TILES, blk_rows)
            return pltpu.make_async_copy(zero_ref, xs_hbm.at[pl.ds(start, blk_rows)], zsem)

        def nonempty(e):
            return pend_ref[e] > (pend_ref[e - 1] if e > 0 else 0)

        nb = xs_hbm.shape[0] // blk_rows

        def tail_copy(j):
            return pltpu.make_async_copy(
                zero_ref, xs_hbm.at[pl.ds((nb - 1 - j) * blk_rows, blk_rows)], zsem)

        def unused(j):
            return (nb - 1 - j) * BM_FFN >= pend_ref[N_EXPERTS - 1]

        for e in range(N_EXPERTS):
            @pl.when(nonempty(e))
            def _():
                zero_copy(e).start()

            @pl.when(unused(e))
            def _():
                tail_copy(e).start()
        for e in range(N_EXPERTS):
            @pl.when(nonempty(e))
            def _():
                zero_copy(e).wait()

            @pl.when(unused(e))
            def _():
                tail_copy(e).wait()

    def row_copy(t, k):
        d = dest_ref[(base + t) * TOP_K + k]
        return pltpu.make_async_copy(hn_ref.at[_row_slice(t)], xs_hbm.at[_row_slice(d)], sem)

    def issue(t, carry):
        for k in range(TOP_K):
            row_copy(t, k).start(priority=k % DMA_QUEUES)
        return carry

    def drain(t, carry):
        for k in range(TOP_K):
            row_copy(t, k).wait()
        return carry

    lax.fori_loop(0, TD_DISPATCH, issue, 0, unroll=8)
    lax.fori_loop(0, TD_DISPATCH, drain, 0, unroll=8)


def _dispatch(dest_flat, pend, hn_rt, nbuf):
    T = hn_rt.shape[0] // ROW_TILES
    return pl.pallas_call(
        _dispatch_kernel,
        grid_spec=pltpu.PrefetchScalarGridSpec(
            num_scalar_prefetch=2,
            grid=(T // TD_DISPATCH,),
            in_specs=[pl.BlockSpec((TD_DISPATCH * ROW_TILES, LANES), lambda i, d, p: (i, 0))],
            out_specs=pl.BlockSpec(memory_space=pl.ANY),
            scratch_shapes=[
                pltpu.VMEM((BM_FFN * ROW_TILES, LANES), F32),
                pltpu.SemaphoreType.DMA,
                pltpu.SemaphoreType.DMA,
            ],
        ),
        out_shape=jax.ShapeDtypeStruct((nbuf * ROW_TILES, LANES), F32),
        compiler_params=_cparams(),
        name="dispatch",
    )(dest_flat, pend, hn_rt)


def _ffn_kernel(bstart_ref, bcnt_ref, xs_hbm, wup_ref, bup_ref, wdn_ref, bdn_ref, y_hbm,
                xbuf, ybuf, wup_bf, wdn_bf, in_sem, out_sem):
    e = pl.program_id(0)
    blk_rows = BM_FFN * ROW_TILES
    b0 = bstart_ref[e]
    cnt = bcnt_ref[e]

    def block_rows(j):
        return pl.ds(pl.multiple_of(j * blk_rows, blk_rows), blk_rows)

    def in_copy(j, slot):
        return pltpu.make_async_copy(xs_hbm.at[block_rows(j)], xbuf.at[slot], in_sem.at[slot])

    def out_copy(j, slot):
        return pltpu.make_async_copy(ybuf.at[slot], y_hbm.at[block_rows(j)], out_sem.at[slot])

    @pl.when(cnt > 0)
    def _():
        in_copy(b0, 0).start(priority=1)
        wup_bf[...] = wup_ref[0].astype(BF16)
        wdn_bf[...] = wdn_ref[0].astype(BF16)

        def body(j, carry):
            slot = j % 2
            in_copy(b0 + j, slot).wait()

            @pl.when(j + 1 < cnt)
            def _():
                in_copy(b0 + j + 1, 1 - slot).start(priority=1)

            @pl.when(j >= 2)
            def _():
                out_copy(b0 + j - 2, slot).wait()

            xb = _load_row_tiled(xbuf.at[slot], BM_FFN).astype(BF16)
            hb = jnp.dot(xb, wup_bf[...], preferred_element_type=F32) + bup_ref[0]
            glu = jnp.minimum(hb[:, 0:D_FF], SWIGLU_LIMIT)
            lin = jnp.clip(hb[:, D_FF:2 * D_FF], -SWIGLU_LIMIT, SWIGLU_LIMIT)
            act = glu * _sigmoid(SWIGLU_ALPHA * glu) * (lin + 1.0)
            y = jnp.dot(act.astype(BF16), wdn_bf[...], preferred_element_type=F32) + bdn_ref[0]
            _store_row_tiled(ybuf.at[slot], y)
            out_copy(b0 + j, slot).start(priority=1)
            return carry

        lax.fori_loop(0, cnt, body, 0)

        @pl.when(cnt >= 2)
        def _():
            out_copy(b0 + cnt - 2, cnt % 2).wait()

        out_copy(b0 + cnt - 1, (cnt - 1) % 2).wait()

    @pl.when(e == N_EXPERTS - 1)
    def _():
        nb = y_hbm.shape[0] // blk_rows
        n_valid = b0 + cnt
        ybuf[0] = jnp.zeros((blk_rows, LANES), F32)
        for j in range(N_EXPERTS):
            @pl.when(nb - 1 - j >= n_valid)
            def _():
                out_copy(nb - 1 - j, 0).start()
        for j in range(N_EXPERTS):
            @pl.when(nb - 1 - j >= n_valid)
            def _():
                out_copy(nb - 1 - j, 0).wait()


def _ffn(bstart, bcnt, xs_rt, w_up, b_up3, w_down, b_down3):
    blk_rows = BM_FFN * ROW_TILES
    wspec = lambda shape: pl.BlockSpec(shape, lambda e, bs, bc: (e, 0, 0))
    return pl.pallas_call(
        _ffn_kernel,
        grid_spec=pltpu.PrefetchScalarGridSpec(
            num_scalar_prefetch=2,
            grid=(N_EXPERTS,),
            in_specs=[
                pl.BlockSpec(memory_space=pl.ANY),
                wspec((1, D_MODEL, 2 * D_FF)),
                wspec((1, 1, 2 * D_FF)),
                wspec((1, D_FF, D_MODEL)),
                wspec((1, 1, D_MODEL)),
            ],
            out_specs=pl.BlockSpec(memory_space=pl.ANY),
            scratch_shapes=[
                pltpu.VMEM((2, blk_rows, LANES), F32),
                pltpu.VMEM((2, blk_rows, LANES), F32),
                pltpu.VMEM((D_MODEL, 2 * D_FF), BF16),
                pltpu.VMEM((D_FF, D_MODEL), BF16),
                pltpu.SemaphoreType.DMA((2,)),
                pltpu.SemaphoreType.DMA((2,)),
            ],
        ),
        out_shape=jax.ShapeDtypeStruct(xs_rt.shape, F32),
        compiler_params=_cparams(),
        name="expert_ffn",
    )(bstart, bcnt, xs_rt, w_up, b_up3, w_down, b_down3)


def _combine_kernel(dest_ref, route_ref, h_ref, gain_ref, yb_hbm, out_ref, buf_ref, sem):
    tm = TM_COMBINE
    base = pl.program_id(0) * tm

    def row_copy(t, k):
        d = dest_ref[(base + t) * TOP_K + k]
        return pltpu.make_async_copy(yb_hbm.at[_row_slice(d)], buf_ref.at[_row_slice(k * tm + t)],
                                     sem)

    def issue(t, carry):
        for k in range(TOP_K):
            row_copy(t, k).start(priority=k % DMA_QUEUES)
        return carry

    def drain(t, carry):
        for k in range(TOP_K):
            row_copy(t, k).wait()
        return carry

    lax.fori_loop(0, tm, issue, 0, unroll=8)
    lax.fori_loop(0, tm, drain, 0, unroll=8)

    r = route_ref[...]
    acc = h_ref[...]
    for k in range(TOP_K):
        acc = acc + r[:, 2 * TOP_K + k:2 * TOP_K + k + 1] * _load_row_tiled(buf_ref, tm, k * tm)
    out = acc * lax.rsqrt(jnp.mean(acc * acc, axis=-1, keepdims=True) + EPS) * gain_ref[...]
    out_ref[...] = out


def _combine(dest_flat, route, h, gain, yb_rt):
    T = h.shape[0]
    tm = TM_COMBINE
    return pl.pallas_call(
        _combine_kernel,
        grid_spec=pltpu.PrefetchScalarGridSpec(
            num_scalar_prefetch=1,
            grid=(T // tm,),
            in_specs=[
                pl.BlockSpec((tm, LANES), lambda i, d: (i, 0)),
                pl.BlockSpec((tm, D_MODEL), lambda i, d: (i, 0)),
                pl.BlockSpec((1, D_MODEL), lambda i, d: (0, 0)),
                pl.BlockSpec(memory_space=pl.ANY),
            ],
            out_specs=pl.BlockSpec((tm, D_MODEL), lambda i, d: (i, 0)),
            scratch_shapes=[
                pltpu.VMEM((TOP_K * tm * ROW_TILES, LANES), F32),
                pltpu.SemaphoreType.DMA,
            ],
        ),
        out_shape=jax.ShapeDtypeStruct((T, D_MODEL), F32),
        compiler_params=_cparams(),
        name="combine",
    )(dest_flat, route, h, gain, yb_rt)


def _layer(x2, B, S, norm1_gain, w_in, conv_qk, b_igate, b_fgate, mlstm_norm_gain, gmlp_ln_gain,
           w_spatial, b_spatial, gmlp_out_gain, w_out, norm2_gain, w_router, b_router,
           w_up, b_up, w_down, b_down):
    T = B * S
    n_gate = 2 * M_HEADS
    g0 = 4 * D_MLSTM
    w_main = jnp.concatenate([w_in[:, :g0], w_in[:, g0 + n_gate:]], axis=1).astype(BF16)
    w_gate = jnp.pad(w_in[:, g0:g0 + n_gate], ((0, 0), (0, LANES - n_gate))).astype(BF16)
    gbias = jnp.pad(jnp.concatenate([b_igate, b_fgate]), (0, LANES - n_gate)).reshape(1, LANES)

    main, gates = _inproj(x2, norm1_gain.reshape(1, D_MODEL), w_main, w_gate)
    h_m = _mlstm(main.reshape(B, S, D_MAIN), gates.reshape(B, S, LANES), conv_qk, gbias,
                 mlstm_norm_gain.reshape(1, D_MLSTM)).reshape(T, D_MLSTM)
    h_g = _gmlp(main, gmlp_ln_gain.reshape(1, D_GMLP), w_spatial, b_spatial.T,
                gmlp_out_gain.reshape(1, D_GMLP))

    w_r = jnp.pad(w_router, ((0, 0), (0, LANES - N_EXPERTS))).astype(BF16)
    b_r = jnp.pad(b_router, (0, LANES - N_EXPERTS)).reshape(1, LANES)
    tril = jnp.tril(jnp.ones((TM_ROUTE, TM_ROUTE), BF16))
    h, hn_rt, route, cnt = _outproj(h_m, h_g, x2, w_out.astype(BF16),
                                    norm2_gain.reshape(1, D_MODEL), w_r, b_r, tril)

    counts = cnt[0, :N_EXPERTS].astype(jnp.int32)
    padded = (counts + BM_FFN - 1) // BM_FFN * BM_FFN
    pend = jnp.cumsum(padded)
    pstart = pend - padded
    nb = (T * TOP_K) // BM_FFN + N_EXPERTS
    pstart_row = jnp.pad(pstart.astype(F32), (0, LANES - N_EXPERTS)).reshape(1, LANES)

    dest = _dest(route, pstart_row)
    dest_flat = dest[:, :TOP_K].reshape(T * TOP_K)

    xs_rt = _dispatch(dest_flat, pend.astype(jnp.int32), hn_rt, nb * BM_FFN)
    yb_rt = _ffn((pstart // BM_FFN).astype(jnp.int32), (padded // BM_FFN).astype(jnp.int32), xs_rt,
                 w_up, b_up.reshape(N_EXPERTS, 1, 2 * D_FF), w_down,
                 b_down.reshape(N_EXPERTS, 1, D_MODEL))
    return dest_flat, route, h, yb_rt


def kernel(x, norm1_gain, w_in, conv_qk, b_igate, b_fgate, mlstm_norm_gain, gmlp_ln_gain,
           w_spatial, b_spatial, gmlp_out_gain, w_out, norm2_gain, w_router, b_router,
           w_up, b_up, w_down, b_down, final_gain):
    B, S, D = x.shape
    depth = norm1_gain.shape[0]
    assert depth == 1 and D == D_MODEL and S % CHUNK == 0
    x2 = x.reshape(B * S, D)
    l = 0
    dest_flat, route, h, yb_rt = _layer(
        x2, B, S, norm1_gain[l], w_in[l], conv_qk[l], b_igate[l], b_fgate[l], mlstm_norm_gain[l],
        gmlp_ln_gain[l], w_spatial[l], b_spatial[l], gmlp_out_gain[l], w_out[l], norm2_gain[l],
        w_router[l], b_router[l], w_up[l], b_up[l], w_down[l], b_down[l])
    out = _combine(dest_flat, route, h, final_gain.reshape(1, D_MODEL), yb_rt)
    return out.reshape(B, S, D)
```

```python
import functools

import jax
import jax.numpy as jnp
from jax import lax
from jax.experimental import pallas as pl
from jax.experimental.pallas import tpu as pltpu

F32 = jnp.float32
BF16 = jnp.bfloat16

D_MODEL = 1024
M_HEADS = 4
M_HEAD_DIM = 128
D_MLSTM = M_HEADS * M_HEAD_DIM
D_GMLP = D_MODEL - D_MLSTM
G_GROUPS = 4
G_DIM = D_GMLP // G_GROUPS
CHUNK = 128
CONV_K = 4
N_EXPERTS = 32
TOP_K = 4
D_FF = D_MODEL
SWIGLU_LIMIT = 7.0
SWIGLU_ALPHA = 1.702
EPS = 1e-6

LANES = 128
SUBLANES = 8
ROW_TILES = D_MODEL // LANES
assert ROW_TILES == SUBLANES
D_MAIN = 4 * D_MLSTM + 2 * D_GMLP
TM_PROJ = 512
TG_GMLP = 512
TM_ROUTE = 512
TD_DISPATCH = 512
TM_COMBINE = 512
BM_FFN = 256
DMA_QUEUES = 2
VMEM_LIMIT = 56 * 1024 * 1024


def _cparams(n_axes=1):
    return pltpu.CompilerParams(
        dimension_semantics=("arbitrary",) * n_axes, vmem_limit_bytes=VMEM_LIMIT)


def _sigmoid(x):
    return 1.0 / (1.0 + jnp.exp(-x))


def _gelu_tanh(x):
    c = 0.7978845608028654
    return x * (0.5 * (1.0 + jnp.tanh(c * (x + 0.044715 * (x * x * x)))))


def _log_sigmoid(x):
    return jnp.minimum(x, 0.0) - jnp.log1p(jnp.exp(-jnp.abs(x)))


def _store_row_tiled(ref, x, row0=0):
    n_rows = x.shape[0]
    for g in range(n_rows // SUBLANES):
        for j in range(ROW_TILES):
            ref[pl.ds((row0 + g * SUBLANES) * ROW_TILES + j, SUBLANES, stride=ROW_TILES), :] = (
                x[g * SUBLANES:(g + 1) * SUBLANES, j * LANES:(j + 1) * LANES])


def _load_row_tiled(ref, n_rows, row0=0):
    groups = []
    for g in range(n_rows // SUBLANES):
        tiles = [ref[pl.ds((row0 + g * SUBLANES) * ROW_TILES + j, SUBLANES, stride=ROW_TILES), :]
                 for j in range(ROW_TILES)]
        groups.append(jnp.concatenate(tiles, axis=1))
    return jnp.concatenate(groups, axis=0)


def _inproj_kernel(x_ref, gain_ref, wm_ref, wg_ref, main_ref, gate_ref):
    x = x_ref[...]
    xn = x * lax.rsqrt(jnp.mean(x * x, axis=-1, keepdims=True) + EPS) * gain_ref[...]
    xb = xn.astype(BF16)
    main_ref[...] = jnp.dot(xb, wm_ref[...], preferred_element_type=F32).astype(BF16)
    gate_ref[...] = jnp.dot(xb, wg_ref[...], preferred_element_type=F32)


def _inproj(x2, gain, w_main, w_gate):
    T = x2.shape[0]
    return pl.pallas_call(
        _inproj_kernel,
        grid=(T // TM_PROJ,),
        in_specs=[
            pl.BlockSpec((TM_PROJ, D_MODEL), lambda i: (i, 0)),
            pl.BlockSpec((1, D_MODEL), lambda i: (0, 0)),
            pl.BlockSpec((D_MODEL, D_MAIN), lambda i: (0, 0)),
            pl.BlockSpec((D_MODEL, LANES), lambda i: (0, 0)),
        ],
        out_specs=[
            pl.BlockSpec((TM_PROJ, D_MAIN), lambda i: (i, 0)),
            pl.BlockSpec((TM_PROJ, LANES), lambda i: (i, 0)),
        ],
        out_shape=[
            jax.ShapeDtypeStruct((T, D_MAIN), BF16),
            jax.ShapeDtypeStruct((T, LANES), F32),
        ],
        compiler_params=_cparams(),
        name="inproj",
    )(x2, gain, w_main, w_gate)


def _mlstm_kernel(qkvo_ref, gates_ref, conv_ref, gbias_ref, ngain_ref, out_ref,
                  xp_ref, ct_ref, n_ref, m_ref, *, batch):
    c = pl.program_id(0)
    L, DH = CHUNK, M_HEAD_DIM

    @pl.when(c == 0)
    def _():
        xp_ref[:, 0:SUBLANES, :] = jnp.zeros((batch, SUBLANES, 2 * D_MLSTM), F32)
        ct_ref[...] = jnp.zeros_like(ct_ref)
        n_ref[...] = jnp.zeros_like(n_ref)
        m_ref[...] = jnp.zeros_like(m_ref)

    row = lax.broadcasted_iota(jnp.int32, (L, L), 0)
    col = lax.broadcasted_iota(jnp.int32, (L, L), 1)
    tril = row >= col
    tril_f = tril.astype(F32)
    chains = [(b, h) for b in range(batch) for h in range(M_HEADS)]


    qk, gc, ball, gc_t, ball_t = [], [], [], [], []
    for b in range(batch):
        xp_ref[b, SUBLANES:SUBLANES + L, :] = qkvo_ref[b, :, 0:2 * D_MLSTM].astype(F32)
        off = SUBLANES - (CONV_K - 1)
        acc = xp_ref[b, off:off + L, :] * conv_ref[0:1, :]
        for j in range(1, CONV_K):
            acc = acc + xp_ref[b, off + j:off + j + L, :] * conv_ref[j:j + 1, :]
        xp_ref[b, 0:SUBLANES, :] = xp_ref[b, L:L + SUBLANES, :]
        qk.append(acc * _sigmoid(acc))
        g = gates_ref[b] + gbias_ref[...]
        gc.append(g)
        ball.append(jnp.dot(tril_f, _log_sigmoid(g), precision=lax.Precision.HIGHEST,
                            preferred_element_type=F32))
    for b in range(batch):
        gc_t.append(gc[b].T)
        ball_t.append(ball[b].T)

    st = []
    for b, h in chains:
        s = b * M_HEADS + h
        q = qk[b][:, h * DH:(h + 1) * DH]
        k = qk[b][:, D_MLSTM + h * DH:D_MLSTM + (h + 1) * DH] * (DH ** -0.5)
        d = dict(
            s=s, b=b, h=h, q=q, k=k, qb=q.astype(BF16), kb=k.astype(BF16),
            v=qkvo_ref[b, :, 2 * D_MLSTM + h * DH:2 * D_MLSTM + (h + 1) * DH],
            b_col=ball[b][:, M_HEADS + h:M_HEADS + h + 1],
            b_row=ball_t[b][M_HEADS + h:M_HEADS + h + 1, :],
            li_col=gc[b][:, h:h + 1],
            li_row=gc_t[b][h:h + 1, :],
            m_prev=m_ref[s][0:1, 0:1],
            ct_prev=ct_ref[s],
            n_prev=n_ref[s],
        )
        d["b_last"] = d["b_col"][L - 1:L, :]
        st.append(d)
    for d in st:
        d["qkt"] = lax.dot_general(d["qb"], d["kb"], (((1,), (1,)), ((), ())),
                                   preferred_element_type=F32)
        d["q_ct"] = jnp.dot(d["qb"], d["ct_prev"].astype(BF16), preferred_element_type=F32)
    for d in st:
        d["dmat"] = jnp.where(tril, d["b_col"] - d["b_row"] + d["li_row"], -jnp.inf)
        d["a_inter"] = d["b_col"] + d["m_prev"]
    for d in st:
        d["m_t"] = jnp.maximum(d["a_inter"], jnp.max(d["dmat"], axis=-1, keepdims=True))

    for d in st:
        d["w"] = jnp.exp(d["dmat"] - d["m_t"]) * d["qkt"]
        d["inter"] = jnp.exp(d["a_inter"] - d["m_t"])
    for d in st:
        d["num"] = (jnp.dot(d["w"].astype(BF16), d["v"], preferred_element_type=F32)
                    + d["inter"] * d["q_ct"])
        d["den"] = (jnp.sum(d["w"], axis=-1, keepdims=True)
                    + d["inter"] * jnp.sum(d["q"] * d["n_prev"], axis=-1, keepdims=True))
    for d in st:
        d["hh"] = d["num"] * (1.0 / jnp.maximum(jnp.abs(d["den"]), jnp.exp(-d["m_t"])))

    for d in st:
        d["mu"] = jnp.mean(d["hh"], axis=-1, keepdims=True)
    for d in st:
        d["xc"] = d["hh"] - d["mu"]
        d["var"] = jnp.mean(d["xc"] * d["xc"], axis=-1, keepdims=True)
    for d in st:
        b, h = d["b"], d["h"]
        y = d["xc"] * lax.rsqrt(d["var"] + EPS) * ngain_ref[:, h * DH:(h + 1) * DH]
        o = qkvo_ref[b, :, 3 * D_MLSTM + h * DH:3 * D_MLSTM + (h + 1) * DH].astype(F32)
        out_ref[b, :, h * DH:(h + 1) * DH] = (_sigmoid(o) * y).astype(out_ref.dtype)

    for d in st:
        d["g_col"] = d["b_last"] - d["b_col"] + d["li_col"]
        d["m_loc"] = jnp.max(d["g_col"], axis=0, keepdims=True)
    for d in st:
        d["w_col"] = jnp.exp(d["g_col"] - d["m_loc"])
        d["k_t"] = d["k"].T.astype(BF16)
    for d in st:
        wv = (d["w_col"] * d["v"].astype(F32)).astype(BF16)
        d["ct_loc"] = jnp.dot(d["k_t"], wv, preferred_element_type=F32)
        d["n_loc"] = jnp.sum(d["w_col"] * d["k"], axis=0, keepdims=True)
    for d in st:
        s = d["s"]
        m_new = jnp.maximum(d["b_last"] + d["m_prev"], d["m_loc"])
        a = jnp.exp(d["b_last"] + d["m_prev"] - m_new)
        cc = jnp.exp(d["m_loc"] - m_new)
        ct_ref[s] = a * d["ct_prev"] + cc * d["ct_loc"]
        n_ref[s] = a * d["n_prev"] + cc * d["n_loc"]
        m_ref[s] = jnp.broadcast_to(m_new, (SUBLANES, LANES))


def _mlstm(main3, gates3, conv_qk, gbias, ngain):
    B, S, _ = main3.shape
    nc = S // CHUNK
    return pl.pallas_call(
        functools.partial(_mlstm_kernel, batch=B),
        grid=(nc,),
        in_specs=[
            pl.BlockSpec((B, CHUNK, 4 * D_MLSTM), lambda c: (0, c, 0)),
            pl.BlockSpec((B, CHUNK, LANES), lambda c: (0, c, 0)),
            pl.BlockSpec((CONV_K, 2 * D_MLSTM), lambda c: (0, 0)),
            pl.BlockSpec((1, LANES), lambda c: (0, 0)),
            pl.BlockSpec((1, D_MLSTM), lambda c: (0, 0)),
        ],
        out_specs=pl.BlockSpec((B, CHUNK, D_MLSTM), lambda c: (0, c, 0)),
        out_shape=jax.ShapeDtypeStruct((B, S, D_MLSTM), BF16),
        scratch_shapes=[
            pltpu.VMEM((B, CHUNK + SUBLANES, 2 * D_MLSTM), F32),
            pltpu.VMEM((B * M_HEADS, M_HEAD_DIM, M_HEAD_DIM), F32),
            pltpu.VMEM((B * M_HEADS, 1, M_HEAD_DIM), F32),
            pltpu.VMEM((B * M_HEADS, SUBLANES, LANES), F32),
        ],
        compiler_params=_cparams(),
        name="mlstm",
    )(main3, gates3, conv_qk, gbias, ngain)


def _gmlp_kernel(uv_ref, lng_ref, ws_ref, bs_ref, og_ref, out_ref, prod_ref):
    L = CHUNK
    u = _gelu_tanh(uv_ref[:, 0:D_GMLP].astype(F32))
    vg = _gelu_tanh(uv_ref[:, D_GMLP:2 * D_GMLP].astype(F32))
    mu = jnp.mean(vg, axis=-1, keepdims=True)
    xc = vg - mu
    vg = xc * lax.rsqrt(jnp.mean(xc * xc, axis=-1, keepdims=True) + EPS) * lng_ref[...]
    vb = vg.astype(BF16)
    row = lax.broadcasted_iota(jnp.int32, (L, L), 0)
    col = lax.broadcasted_iota(jnp.int32, (L, L), 1)
    tril = row >= col
    for g in range(G_GROUPS):
        wg = jnp.where(tril, ws_ref[g], 0.0).astype(BF16)
        b_col = bs_ref[:, g:g + 1]
        for j in range(TG_GMLP // L):
            mixed = jnp.dot(wg, vb[j * L:(j + 1) * L, g * G_DIM:(g + 1) * G_DIM],
                            preferred_element_type=F32) + b_col
            prod_ref[j * L:(j + 1) * L, g * G_DIM:(g + 1) * G_DIM] = (
                u[j * L:(j + 1) * L, g * G_DIM:(g + 1) * G_DIM] * mixed)
    p = prod_ref[...]
    y = p * lax.rsqrt(jnp.mean(p * p, axis=-1, keepdims=True) + EPS) * og_ref[...]
    out_ref[...] = y.astype(out_ref.dtype)


def _gmlp(main, ln_gain, w_spatial, b_spatial_t, out_gain):
    T = main.shape[0]
    uv_block = 4 * D_MLSTM // (2 * D_GMLP)
    return pl.pallas_call(
        _gmlp_kernel,
        grid=(T // TG_GMLP,),
        in_specs=[
            pl.BlockSpec((TG_GMLP, 2 * D_GMLP), lambda i: (i, uv_block)),
            pl.BlockSpec((1, D_GMLP), lambda i: (0, 0)),
            pl.BlockSpec((G_GROUPS, CHUNK, CHUNK), lambda i: (0, 0, 0)),
            pl.BlockSpec((CHUNK, G_GROUPS), lambda i: (0, 0)),
            pl.BlockSpec((1, D_GMLP), lambda i: (0, 0)),
        ],
        out_specs=pl.BlockSpec((TG_GMLP, D_GMLP), lambda i: (i, 0)),
        out_shape=jax.ShapeDtypeStruct((T, D_GMLP), BF16),
        scratch_shapes=[pltpu.VMEM((TG_GMLP, D_GMLP), F32)],
        compiler_params=_cparams(),
        name="gmlp",
    )(main, ln_gain, w_spatial, b_spatial_t, out_gain)


def _outproj_kernel(hm_ref, hg_ref, x_ref, wo_ref, g2_ref, wr_ref, br_ref, tril_ref,
                    h_ref, hn_ref, route_ref, cnt_ref, carry_ref):
    i = pl.program_id(0)
    tm = TM_ROUTE

    @pl.when(i == 0)
    def _():
        carry_ref[...] = jnp.zeros_like(carry_ref)

    y = (jnp.dot(hm_ref[...], wo_ref[0:D_MLSTM, :], preferred_element_type=F32)
         + jnp.dot(hg_ref[...], wo_ref[D_MLSTM:D_MODEL, :], preferred_element_type=F32))
    h = x_ref[...] + y
    h_ref[...] = h
    hn = h * lax.rsqrt(jnp.mean(h * h, axis=-1, keepdims=True) + EPS) * g2_ref[...]
    _store_row_tiled(hn_ref, hn)

    logits = jnp.dot(hn.astype(BF16), wr_ref[...], preferred_element_type=F32) + br_ref[...]
    lane = lax.broadcasted_iota(jnp.int32, (tm, LANES), 1).astype(F32)
    l = jnp.where(lane < N_EXPERTS, logits, -jnp.inf)
    top_v, top_i = [], []
    for _ in range(TOP_K):
        mx = jnp.max(l, axis=-1, keepdims=True)
        ix = jnp.min(jnp.where(l == mx, lane, float(LANES)), axis=-1, keepdims=True)
        top_v.append(mx)
        top_i.append(ix)
        l = jnp.where(lane == ix, -jnp.inf, l)
    ex = [jnp.exp(v - top_v[0]) for v in top_v]
    inv = 1.0 / (ex[0] + ex[1] + ex[2] + ex[3])

    onehot = [lane == ix for ix in top_i]
    mask = (onehot[0].astype(F32) + onehot[1].astype(F32)
            + onehot[2].astype(F32) + onehot[3].astype(F32))
    incl = jnp.dot(tril_ref[...], mask.astype(BF16), preferred_element_type=F32)
    base = carry_ref[0:1, :] + incl - mask
    route = jnp.zeros((tm, LANES), F32)
    for k in range(TOP_K):
        rank = jnp.sum(jnp.where(onehot[k], base, 0.0), axis=-1, keepdims=True)
        route = jnp.where(lane == k, top_i[k], route)
        route = jnp.where(lane == TOP_K + k, rank, route)
        route = jnp.where(lane == 2 * TOP_K + k, ex[k] * inv, route)
    route_ref[...] = route
    carry_ref[...] = carry_ref[...] + incl[tm - 1:tm, :]
    cnt_ref[...] = carry_ref[...]


def _outproj(h_m, h_g, x2, w_out, gain2, w_router, b_router, tril):
    T = x2.shape[0]
    tm = TM_ROUTE
    return pl.pallas_call(
        _outproj_kernel,
        grid=(T // tm,),
        in_specs=[
            pl.BlockSpec((tm, D_MLSTM), lambda i: (i, 0)),
            pl.BlockSpec((tm, D_GMLP), lambda i: (i, 0)),
            pl.BlockSpec((tm, D_MODEL), lambda i: (i, 0)),
            pl.BlockSpec((D_MODEL, D_MODEL), lambda i: (0, 0)),
            pl.BlockSpec((1, D_MODEL), lambda i: (0, 0)),
            pl.BlockSpec((D_MODEL, LANES), lambda i: (0, 0)),
            pl.BlockSpec((1, LANES), lambda i: (0, 0)),
            pl.BlockSpec((tm, tm), lambda i: (0, 0)),
        ],
        out_specs=[
            pl.BlockSpec((tm, D_MODEL), lambda i: (i, 0)),
            pl.BlockSpec((tm * ROW_TILES, LANES), lambda i: (i, 0)),
            pl.BlockSpec((tm, LANES), lambda i: (i, 0)),
            pl.BlockSpec((SUBLANES, LANES), lambda i: (0, 0)),
        ],
        out_shape=[
            jax.ShapeDtypeStruct((T, D_MODEL), F32),
            jax.ShapeDtypeStruct((T * ROW_TILES, LANES), F32),
            jax.ShapeDtypeStruct((T, LANES), F32),
            jax.ShapeDtypeStruct((SUBLANES, LANES), F32),
        ],
        scratch_shapes=[pltpu.VMEM((SUBLANES, LANES), F32)],
        compiler_params=_cparams(),
        name="outproj_router",
    )(h_m, h_g, x2, w_out, gain2, w_router, b_router, tril)


def _dest_kernel(route_ref, ptab_ref, dest_ref):
    r = route_ref[...]
    tm = r.shape[0]
    lane = lax.broadcasted_iota(jnp.int32, (tm, LANES), 1).astype(F32)
    out = jnp.zeros((tm, LANES), F32)
    for k in range(TOP_K):
        ix = r[:, k:k + 1]
        rank = r[:, TOP_K + k:TOP_K + k + 1]
        start = jnp.dot((lane == ix).astype(F32), ptab_ref[...],
                        precision=lax.Precision.HIGHEST, preferred_element_type=F32)
        out = jnp.where(lane == k, rank + start, out)
    dest_ref[...] = out.astype(jnp.int32)


def _dest(route, pstart):
    T = route.shape[0]
    tm = TM_ROUTE
    return pl.pallas_call(
        _dest_kernel,
        grid=(T // tm,),
        in_specs=[
            pl.BlockSpec((tm, LANES), lambda i: (i, 0)),
            pl.BlockSpec((LANES, LANES), lambda i: (0, 0)),
        ],
        out_specs=pl.BlockSpec((tm, LANES), lambda i: (i, 0)),
        out_shape=jax.ShapeDtypeStruct((T, LANES), jnp.int32),
        compiler_params=_cparams(),
        name="dest",
    )(route, pstart)


def _row_slice(n):
    return pl.ds(pl.multiple_of(n * ROW_TILES, ROW_TILES), ROW_TILES)


def _dispatch_kernel(dest_ref, pend_ref, hn_ref, xs_hbm, zero_ref, sem, zsem):
    i = pl.program_id(0)
    base = i * TD_DISPATCH
    blk_rows = BM_FFN * ROW_TILES

    @pl.when(i == 0)
    def _():
        zero_ref[...] = jnp.zeros_like(zero_ref)

        def zero_copy(e):
            start = pl.multiple_of((pend_ref[e] - BM_FFN) * ROW_TILES, blk_rows)
            return pltpu.make_async_copy(zero_ref, xs_hbm.at[pl.ds(start, blk_rows)], zsem)

        def nonempty(e):
            return pend_ref[e] > (pend_ref[e - 1] if e > 0 else 0)

        nb = xs_hbm.shape[0] // blk_rows

        def tail_copy(j):
            return pltpu.make_async_copy(
                zero_ref, xs_hbm.at[pl.ds((nb - 1 - j) * blk_rows, blk_rows)], zsem)

        def unused(j):
            return (nb - 1 - j) * BM_FFN >= pend_ref[N_EXPERTS - 1]

        for e in range(N_EXPERTS):
            @pl.when(nonempty(e))
            def _():
                zero_copy(e).start()

            @pl.when(unused(e))
            def _():
                tail_copy(e).start()
        for e in range(N_EXPERTS):
            @pl.when(nonempty(e))
            def _():
                zero_copy(e).wait()

            @pl.when(unused(e))
            def _():
                tail_copy(e).wait()

    def row_copy(t, k):
        d = dest_ref[(base + t) * TOP_K + k]
        return pltpu.make_async_copy(hn_ref.at[_row_slice(t)], xs_hbm.at[_row_slice(d)], sem)

    def issue(t, carry):
        for k in range(TOP_K):
            row_copy(t, k).start(priority=k % DMA_QUEUES)
        return carry

    def drain(t, carry):
        for k in range(TOP_K):
            row_copy(t, k).wait()
        return carry

    lax.fori_loop(0, TD_DISPATCH, issue, 0, unroll=8)
    lax.fori_loop(0, TD_DISPATCH, drain, 0, unroll=8)


def _dispatch(dest_flat, pend, hn_rt, nbuf):
    T = hn_rt.shape[0] // ROW_TILES
    return pl.pallas_call(
        _dispatch_kernel,
        grid_spec=pltpu.PrefetchScalarGridSpec(
            num_scalar_prefetch=2,
            grid=(T // TD_DISPATCH,),
            in_specs=[pl.BlockSpec((TD_DISPATCH * ROW_TILES, LANES), lambda i, d, p: (i, 0))],
            out_specs=pl.BlockSpec(memory_space=pl.ANY),
            scratch_shapes=[
                pltpu.VMEM((BM_FFN * ROW_TILES, LANES), F32),
                pltpu.SemaphoreType.DMA,
                pltpu.SemaphoreType.DMA,
            ],
        ),
        out_shape=jax.ShapeDtypeStruct((nbuf * ROW_TILES, LANES), F32),
        compiler_params=_cparams(),
        name="dispatch",
    )(dest_flat, pend, hn_rt)


def _ffn_kernel(bstart_ref, bcnt_ref, xs_hbm, wup_ref, bup_ref, wdn_ref, bdn_ref, y_hbm,
                xbuf, ybuf, wup_bf, wdn_bf, in_sem, out_sem):
    e = pl.program_id(0)
    blk_rows = BM_FFN * ROW_TILES
    b0 = bstart_ref[e]
    cnt = bcnt_ref[e]

    def block_rows(j):
        return pl.ds(pl.multiple_of(j * blk_rows, blk_rows), blk_rows)

    def in_copy(j, slot):
        return pltpu.make_async_copy(xs_hbm.at[block_rows(j)], xbuf.at[slot], in_sem.at[slot])

    def out_copy(j, slot):
        return pltpu.make_async_copy(ybuf.at[slot], y_hbm.at[block_rows(j)], out_sem.at[slot])

    @pl.when(cnt > 0)
    def _():
        in_copy(b0, 0).start(priority=1)
        wup_bf[...] = wup_ref[0].astype(BF16)
        wdn_bf[...] = wdn_ref[0].astype(BF16)

        def body(j, carry):
            slot = j % 2
            in_copy(b0 + j, slot).wait()

            @pl.when(j + 1 < cnt)
            def _():
                in_copy(b0 + j + 1, 1 - slot).start(priority=1)

            @pl.when(j >= 2)
            def _():
                out_copy(b0 + j - 2, slot).wait()

            xb = _load_row_tiled(xbuf.at[slot], BM_FFN).astype(BF16)
            hb = jnp.dot(xb, wup_bf[...], preferred_element_type=F32) + bup_ref[0]
            glu = jnp.minimum(hb[:, 0:D_FF], SWIGLU_LIMIT)
            lin = jnp.clip(hb[:, D_FF:2 * D_FF], -SWIGLU_LIMIT, SWIGLU_LIMIT)
            act = glu * _sigmoid(SWIGLU_ALPHA * glu) * (lin + 1.0)
            y = jnp.dot(act.astype(BF16), wdn_bf[...], preferred_element_type=F32) + bdn_ref[0]
            _store_row_tiled(ybuf.at[slot], y)
            out_copy(b0 + j, slot).start(priority=1)
            return carry

        lax.fori_loop(0, cnt, body, 0)

        @pl.when(cnt >= 2)
        def _():
            out_copy(b0 + cnt - 2, cnt % 2).wait()

        out_copy(b0 + cnt - 1, (cnt - 1) % 2).wait()

    @pl.when(e == N_EXPERTS - 1)
    def _():
        nb = y_hbm.shape[0] // blk_rows
        n_valid = b0 + cnt
        ybuf[0] = jnp.zeros((blk_rows, LANES), F32)
        for j in range(N_EXPERTS):
            @pl.when(nb - 1 - j >= n_valid)
            def _():
                out_copy(nb - 1 - j, 0).start()
        for j in range(N_EXPERTS):
            @pl.when(nb - 1 - j >= n_valid)
            def _():
                out_copy(nb - 1 - j, 0).wait()


def _ffn(bstart, bcnt, xs_rt, w_up, b_up3, w_down, b_down3):
    blk_rows = BM_FFN * ROW_TILES
    wspec = lambda shape: pl.BlockSpec(shape, lambda e, bs, bc: (e, 0, 0))
    return pl.pallas_call(
        _ffn_kernel,
        grid_spec=pltpu.PrefetchScalarGridSpec(
            num_scalar_prefetch=2,
            grid=(N_EXPERTS,),
            in_specs=[
                pl.BlockSpec(memory_space=pl.ANY),
                wspec((1, D_MODEL, 2 * D_FF)),
                wspec((1, 1, 2 * D_FF)),
                wspec((1, D_FF, D_MODEL)),
                wspec((1, 1, D_MODEL)),
            ],
            out_specs=pl.BlockSpec(memory_space=pl.ANY),
            scratch_shapes=[
                pltpu.VMEM((2, blk_rows, LANES), F32),
                pltpu.VMEM((2, blk_rows, LANES), F32),
                pltpu.VMEM((D_MODEL, 2 * D_FF), BF16),
                pltpu.VMEM((D_FF, D_MODEL), BF16),
                pltpu.SemaphoreType.DMA((2,)),
                pltpu.SemaphoreType.DMA((2,)),
            ],
        ),
        out_shape=jax.ShapeDtypeStruct(xs_rt.shape, F32),
        compiler_params=_cparams(),
        name="expert_ffn",
    )(bstart, bcnt, xs_rt, w_up, b_up3, w_down, b_down3)


def _combine_kernel(dest_ref, route_ref, h_ref, gain_ref, yb_hbm, out_ref, buf_ref, sem):
    tm = TM_COMBINE
    i = pl.program_id(0)

    def row_copy(step, t, k):
        d = dest_ref[(step * tm + t) * TOP_K + k]
        slot = step % 2
        return pltpu.make_async_copy(
            yb_hbm.at[_row_slice(d)], buf_ref.at[slot, _row_slice(k * tm + t)], sem.at[slot])

    def issue(step):
        def body(t, carry):
            for k in range(TOP_K):
                row_copy(step, t, k).start(priority=k % DMA_QUEUES)
            return carry
        lax.fori_loop(0, tm, body, 0, unroll=8)

    def drain(step):
        def body(t, carry):
            for k in range(TOP_K):
                row_copy(step, t, k).wait()
            return carry
        lax.fori_loop(0, tm, body, 0, unroll=8)

    @pl.when(i == 0)
    def _():
        issue(i)

    @pl.when(i + 1 < pl.num_programs(0))
    def _():
        issue(i + 1)

    drain(i)

    r = route_ref[...]
    acc = h_ref[...]
    cur = buf_ref.at[i % 2]
    for k in range(TOP_K):
        acc = acc + r[:, 2 * TOP_K + k:2 * TOP_K + k + 1] * _load_row_tiled(cur, tm, k * tm)
    out = acc * lax.rsqrt(jnp.mean(acc * acc, axis=-1, keepdims=True) + EPS) * gain_ref[...]
    out_ref[...] = out


def _combine(dest_flat, route, h, gain, yb_rt):
    T = h.shape[0]
    tm = TM_COMBINE
    return pl.pallas_call(
        _combine_kernel,
        grid_spec=pltpu.PrefetchScalarGridSpec(
            num_scalar_prefetch=1,
            grid=(T // tm,),
            in_specs=[
                pl.BlockSpec((tm, LANES), lambda i, d: (i, 0)),
                pl.BlockSpec((tm, D_MODEL), lambda i, d: (i, 0)),
                pl.BlockSpec((1, D_MODEL), lambda i, d: (0, 0)),
                pl.BlockSpec(memory_space=pl.ANY),
            ],
            out_specs=pl.BlockSpec((tm, D_MODEL), lambda i, d: (i, 0)),
            scratch_shapes=[
                pltpu.VMEM((2, TOP_K * tm * ROW_TILES, LANES), F32),
                pltpu.SemaphoreType.DMA((2,)),
            ],
        ),
        out_shape=jax.ShapeDtypeStruct((T, D_MODEL), F32),
        compiler_params=_cparams(),
        name="combine",
    )(dest_flat, route, h, gain, yb_rt)


def _layer(x2, B, S, norm1_gain, w_in, conv_qk, b_igate, b_fgate, mlstm_norm_gain, gmlp_ln_gain,
           w_spatial, b_spatial, gmlp_out_gain, w_out, norm2_gain, w_router, b_router,
           w_up, b_up, w_down, b_down):
    T = B * S
    n_gate = 2 * M_HEADS
    g0 = 4 * D_MLSTM
    w_main = jnp.concatenate([w_in[:, :g0], w_in[:, g0 + n_gate:]], axis=1).astype(BF16)
    w_gate = jnp.pad(w_in[:, g0:g0 + n_gate], ((0, 0), (0, LANES - n_gate))).astype(BF16)
    gbias = jnp.pad(jnp.concatenate([b_igate, b_fgate]), (0, LANES - n_gate)).reshape(1, LANES)

    main, gates = _inproj(x2, norm1_gain.reshape(1, D_MODEL), w_main, w_gate)
    h_m = _mlstm(main.reshape(B, S, D_MAIN), gates.reshape(B, S, LANES), conv_qk, gbias,
                 mlstm_norm_gain.reshape(1, D_MLSTM)).reshape(T, D_MLSTM)
    h_g = _gmlp(main, gmlp_ln_gain.reshape(1, D_GMLP), w_spatial, b_spatial.T,
                gmlp_out_gain.reshape(1, D_GMLP))

    w_r = jnp.pad(w_router, ((0, 0), (0, LANES - N_EXPERTS))).astype(BF16)
    b_r = jnp.pad(b_router, (0, LANES - N_EXPERTS)).reshape(1, LANES)
    tril = jnp.tril(jnp.ones((TM_ROUTE, TM_ROUTE), BF16))
    h, hn_rt, route, cnt = _outproj(h_m, h_g, x2, w_out.astype(BF16),
                                    norm2_gain.reshape(1, D_MODEL), w_r, b_r, tril)

    counts = cnt[0, :N_EXPERTS].astype(jnp.int32)
    padded = (counts + BM_FFN - 1) // BM_FFN * BM_FFN
    pend = jnp.cumsum(padded)
    pstart = pend - padded
    nb = (T * TOP_K) // BM_FFN + N_EXPERTS
    pstart_tab = jnp.broadcast_to(
        jnp.pad(pstart.astype(F32), (0, LANES - N_EXPERTS))[:, None], (LANES, LANES))

    dest = _dest(route, pstart_tab)
    dest_flat = dest[:, :TOP_K].reshape(T * TOP_K)

    xs_rt = _dispatch(dest_flat, pend.astype(jnp.int32), hn_rt, nb * BM_FFN)
    yb_rt = _ffn((pstart // BM_FFN).astype(jnp.int32), (padded // BM_FFN).astype(jnp.int32), xs_rt,
                 w_up, b_up.reshape(N_EXPERTS, 1, 2 * D_FF), w_down,
                 b_down.reshape(N_EXPERTS, 1, D_MODEL))
    return dest_flat, route, h, yb_rt


def kernel(x, norm1_gain, w_in, conv_qk, b_igate, b_fgate, mlstm_norm_gain, gmlp_ln_gain,
           w_spatial, b_spatial, gmlp_out_gain, w_out, norm2_gain, w_router, b_router,
           w_up, b_up, w_down, b_down, final_gain):
    B, S, D = x.shape
    depth = norm1_gain.shape[0]
    assert depth == 1 and D == D_MODEL and S % CHUNK == 0
    x2 = x.reshape(B * S, D)
    l = 0
    dest_flat, route, h, yb_rt = _layer(
        x2, B, S, norm1_gain[l], w_in[l], conv_qk[l], b_igate[l], b_fgate[l], mlstm_norm_gain[l],
        gmlp_ln_gain[l], w_spatial[l], b_spatial[l], gmlp_out_gain[l], w_out[l], norm2_gain[l],
        w_router[l], b_router[l], w_up[l], b_up[l], w_down[l], b_down[l])
    out = _combine(dest_flat, route, h, final_gain.reshape(1, D_MODEL), yb_rt)
    return out.reshape(B, S, D)
```

```python
import functools

import jax
import jax.numpy as jnp
from jax import lax
from jax.experimental import pallas as pl
from jax.experimental.pallas import tpu as pltpu

F32 = jnp.float32
BF16 = jnp.bfloat16

D_MODEL = 1024
M_HEADS = 4
M_HEAD_DIM = 128
D_MLSTM = M_HEADS * M_HEAD_DIM
D_GMLP = D_MODEL - D_MLSTM
G_GROUPS = 4
G_DIM = D_GMLP // G_GROUPS
CHUNK = 128
CONV_K = 4
N_EXPERTS = 32
TOP_K = 4
D_FF = D_MODEL
SWIGLU_LIMIT = 7.0
SWIGLU_ALPHA = 1.702
EPS = 1e-6

LANES = 128
SUBLANES = 8
ROW_TILES = D_MODEL // LANES
assert ROW_TILES == SUBLANES
D_MAIN = 4 * D_MLSTM + 2 * D_GMLP
TM_PROJ = 512
TG_GMLP = 512
TM_ROUTE = 512
TD_DISPATCH = 512
TM_COMBINE = 512
BM_FFN = 256
DMA_QUEUES = 2
VMEM_LIMIT = 56 * 1024 * 1024


def _cparams(n_axes=1):
    return pltpu.CompilerParams(
        dimension_semantics=("arbitrary",) * n_axes, vmem_limit_bytes=VMEM_LIMIT)


def _sigmoid(x):
    return 1.0 / (1.0 + jnp.exp(-x))


def _gelu_tanh(x):
    c = 0.7978845608028654
    return x * (0.5 * (1.0 + jnp.tanh(c * (x + 0.044715 * (x * x * x)))))


def _log_sigmoid(x):
    return jnp.minimum(x, 0.0) - jnp.log1p(jnp.exp(-jnp.abs(x)))


def _store_row_tiled(ref, x, row0=0):
    n_rows = x.shape[0]
    for g in range(n_rows // SUBLANES):
        for j in range(ROW_TILES):
            ref[pl.ds((row0 + g * SUBLANES) * ROW_TILES + j, SUBLANES, stride=ROW_TILES), :] = (
                x[g * SUBLANES:(g + 1) * SUBLANES, j * LANES:(j + 1) * LANES])


def _load_row_tiled(ref, n_rows, row0=0):
    groups = []
    for g in range(n_rows // SUBLANES):
        tiles = [ref[pl.ds((row0 + g * SUBLANES) * ROW_TILES + j, SUBLANES, stride=ROW_TILES), :]
                 for j in range(ROW_TILES)]
        groups.append(jnp.concatenate(tiles, axis=1))
    return jnp.concatenate(groups, axis=0)


def _inproj_kernel(x_ref, gain_ref, wm_ref, wg_ref, main_ref, gate_ref):
    x = x_ref[...]
    xn = x * lax.rsqrt(jnp.mean(x * x, axis=-1, keepdims=True) + EPS) * gain_ref[...]
    xb = xn.astype(BF16)
    main_ref[...] = jnp.dot(xb, wm_ref[...], preferred_element_type=F32).astype(BF16)
    gate_ref[...] = jnp.dot(xb, wg_ref[...], preferred_element_type=F32)


def _inproj(x2, gain, w_main, w_gate):
    T = x2.shape[0]
    return pl.pallas_call(
        _inproj_kernel,
        grid=(T // TM_PROJ,),
        in_specs=[
            pl.BlockSpec((TM_PROJ, D_MODEL), lambda i: (i, 0)),
            pl.BlockSpec((1, D_MODEL), lambda i: (0, 0)),
            pl.BlockSpec((D_MODEL, D_MAIN), lambda i: (0, 0)),
            pl.BlockSpec((D_MODEL, LANES), lambda i: (0, 0)),
        ],
        out_specs=[
            pl.BlockSpec((TM_PROJ, D_MAIN), lambda i: (i, 0)),
            pl.BlockSpec((TM_PROJ, LANES), lambda i: (i, 0)),
        ],
        out_shape=[
            jax.ShapeDtypeStruct((T, D_MAIN), BF16),
            jax.ShapeDtypeStruct((T, LANES), F32),
        ],
        compiler_params=_cparams(),
        name="inproj",
    )(x2, gain, w_main, w_gate)


def _mlstm_kernel(qkvo_ref, gates_ref, conv_ref, gbias_ref, ngain_ref, out_ref,
                  xp_ref, ct_ref, n_ref, m_ref, *, batch):
    c = pl.program_id(0)
    L, DH = CHUNK, M_HEAD_DIM

    @pl.when(c == 0)
    def _():
        xp_ref[:, 0:SUBLANES, :] = jnp.zeros((batch, SUBLANES, 2 * D_MLSTM), F32)
        ct_ref[...] = jnp.zeros_like(ct_ref)
        n_ref[...] = jnp.zeros_like(n_ref)
        m_ref[...] = jnp.zeros_like(m_ref)

    row = lax.broadcasted_iota(jnp.int32, (L, L), 0)
    col = lax.broadcasted_iota(jnp.int32, (L, L), 1)
    tril = row >= col
    tril_f = tril.astype(F32)
    chains = [(b, h) for b in range(batch) for h in range(M_HEADS)]


    qk, gc, ball, gc_t, ball_t = [], [], [], [], []
    for b in range(batch):
        xp_ref[b, SUBLANES:SUBLANES + L, :] = qkvo_ref[b, :, 0:2 * D_MLSTM].astype(F32)
        off = SUBLANES - (CONV_K - 1)
        acc = xp_ref[b, off:off + L, :] * conv_ref[0:1, :]
        for j in range(1, CONV_K):
            acc = acc + xp_ref[b, off + j:off + j + L, :] * conv_ref[j:j + 1, :]
        xp_ref[b, 0:SUBLANES, :] = xp_ref[b, L:L + SUBLANES, :]
        qk.append(acc * _sigmoid(acc))
        g = gates_ref[b] + gbias_ref[...]
        gc.append(g)
        ball.append(jnp.dot(tril_f, _log_sigmoid(g), precision=lax.Precision.HIGHEST,
                            preferred_element_type=F32))
    for b in range(batch):
        gc_t.append(gc[b].T)
        ball_t.append(ball[b].T)

    st = []
    for b, h in chains:
        s = b * M_HEADS + h
        q = qk[b][:, h * DH:(h + 1) * DH]
        k = qk[b][:, D_MLSTM + h * DH:D_MLSTM + (h + 1) * DH] * (DH ** -0.5)
        d = dict(
            s=s, b=b, h=h, q=q, k=k, qb=q.astype(BF16), kb=k.astype(BF16),
            v=qkvo_ref[b, :, 2 * D_MLSTM + h * DH:2 * D_MLSTM + (h + 1) * DH],
            b_col=ball[b][:, M_HEADS + h:M_HEADS + h + 1],
            b_row=ball_t[b][M_HEADS + h:M_HEADS + h + 1, :],
            li_col=gc[b][:, h:h + 1],
            li_row=gc_t[b][h:h + 1, :],
            m_prev=m_ref[s][0:1, 0:1],
            ct_prev=ct_ref[s],
            n_prev=n_ref[s],
        )
        d["b_last"] = d["b_col"][L - 1:L, :]
        st.append(d)
    for d in st:
        d["qkt"] = lax.dot_general(d["qb"], d["kb"], (((1,), (1,)), ((), ())),
                                   preferred_element_type=F32)
        d["q_ct"] = jnp.dot(d["qb"], d["ct_prev"].astype(BF16), preferred_element_type=F32)
    for d in st:
        d["dmat"] = jnp.where(tril, d["b_col"] - d["b_row"] + d["li_row"], -jnp.inf)
        d["a_inter"] = d["b_col"] + d["m_prev"]
    for d in st:
        d["m_t"] = jnp.maximum(d["a_inter"], jnp.max(d["dmat"], axis=-1, keepdims=True))

    for d in st:
        d["w"] = jnp.exp(d["dmat"] - d["m_t"]) * d["qkt"]
        d["inter"] = jnp.exp(d["a_inter"] - d["m_t"])
    for d in st:
        d["num"] = (jnp.dot(d["w"].astype(BF16), d["v"], preferred_element_type=F32)
                    + d["inter"] * d["q_ct"])
        d["den"] = (jnp.sum(d["w"], axis=-1, keepdims=True)
                    + d["inter"] * jnp.sum(d["q"] * d["n_prev"], axis=-1, keepdims=True))
    for d in st:
        d["hh"] = d["num"] * (1.0 / jnp.maximum(jnp.abs(d["den"]), jnp.exp(-d["m_t"])))

    for d in st:
        d["mu"] = jnp.mean(d["hh"], axis=-1, keepdims=True)
    for d in st:
        d["xc"] = d["hh"] - d["mu"]
        d["var"] = jnp.mean(d["xc"] * d["xc"], axis=-1, keepdims=True)
    for d in st:
        b, h = d["b"], d["h"]
        y = d["xc"] * lax.rsqrt(d["var"] + EPS) * ngain_ref[:, h * DH:(h + 1) * DH]
        o = qkvo_ref[b, :, 3 * D_MLSTM + h * DH:3 * D_MLSTM + (h + 1) * DH].astype(F32)
        out_ref[b, :, h * DH:(h + 1) * DH] = (_sigmoid(o) * y).astype(out_ref.dtype)

    for d in st:
        d["g_col"] = d["b_last"] - d["b_col"] + d["li_col"]
        d["m_loc"] = jnp.max(d["g_col"], axis=0, keepdims=True)
    for d in st:
        d["w_col"] = jnp.exp(d["g_col"] - d["m_loc"])
        d["k_t"] = d["k"].T.astype(BF16)
    for d in st:
        wv = (d["w_col"] * d["v"].astype(F32)).astype(BF16)
        d["ct_loc"] = jnp.dot(d["k_t"], wv, preferred_element_type=F32)
        d["n_loc"] = jnp.sum(d["w_col"] * d["k"], axis=0, keepdims=True)
    for d in st:
        s = d["s"]
        m_new = jnp.maximum(d["b_last"] + d["m_prev"], d["m_loc"])
        a = jnp.exp(d["b_last"] + d["m_prev"] - m_new)
        cc = jnp.exp(d["m_loc"] - m_new)
        ct_ref[s] = a * d["ct_prev"] + cc * d["ct_loc"]
        n_ref[s] = a * d["n_prev"] + cc * d["n_loc"]
        m_ref[s] = jnp.broadcast_to(m_new, (SUBLANES, LANES))


def _mlstm(main3, gates3, conv_qk, gbias, ngain):
    B, S, _ = main3.shape
    nc = S // CHUNK
    return pl.pallas_call(
        functools.partial(_mlstm_kernel, batch=B),
        grid=(nc,),
        in_specs=[
            pl.BlockSpec((B, CHUNK, 4 * D_MLSTM), lambda c: (0, c, 0)),
            pl.BlockSpec((B, CHUNK, LANES), lambda c: (0, c, 0)),
            pl.BlockSpec((CONV_K, 2 * D_MLSTM), lambda c: (0, 0)),
            pl.BlockSpec((1, LANES), lambda c: (0, 0)),
            pl.BlockSpec((1, D_MLSTM), lambda c: (0, 0)),
        ],
        out_specs=pl.BlockSpec((B, CHUNK, D_MLSTM), lambda c: (0, c, 0)),
        out_shape=jax.ShapeDtypeStruct((B, S, D_MLSTM), BF16),
        scratch_shapes=[
            pltpu.VMEM((B, CHUNK + SUBLANES, 2 * D_MLSTM), F32),
            pltpu.VMEM((B * M_HEADS, M_HEAD_DIM, M_HEAD_DIM), F32),
            pltpu.VMEM((B * M_HEADS, 1, M_HEAD_DIM), F32),
            pltpu.VMEM((B * M_HEADS, SUBLANES, LANES), F32),
        ],
        compiler_params=_cparams(),
        name="mlstm",
    )(main3, gates3, conv_qk, gbias, ngain)


def _gmlp_kernel(uv_ref, lng_ref, ws_ref, bs_ref, og_ref, out_ref, prod_ref):
    L = CHUNK
    u = _gelu_tanh(uv_ref[:, 0:D_GMLP].astype(F32))
    vg = _gelu_tanh(uv_ref[:, D_GMLP:2 * D_GMLP].astype(F32))
    mu = jnp.mean(vg, axis=-1, keepdims=True)
    xc = vg - mu
    vg = xc * lax.rsqrt(jnp.mean(xc * xc, axis=-1, keepdims=True) + EPS) * lng_ref[...]
    vb = vg.astype(BF16)
    row = lax.broadcasted_iota(jnp.int32, (L, L), 0)
    col = lax.broadcasted_iota(jnp.int32, (L, L), 1)
    tril = row >= col
    for g in range(G_GROUPS):
        wg = jnp.where(tril, ws_ref[g], 0.0).astype(BF16)
        b_col = bs_ref[:, g:g + 1]
        for j in range(TG_GMLP // L):
            mixed = jnp.dot(wg, vb[j * L:(j + 1) * L, g * G_DIM:(g + 1) * G_DIM],
                            preferred_element_type=F32) + b_col
            prod_ref[j * L:(j + 1) * L, g * G_DIM:(g + 1) * G_DIM] = (
                u[j * L:(j + 1) * L, g * G_DIM:(g + 1) * G_DIM] * mixed)
    p = prod_ref[...]
    y = p * lax.rsqrt(jnp.mean(p * p, axis=-1, keepdims=True) + EPS) * og_ref[...]
    out_ref[...] = y.astype(out_ref.dtype)


def _gmlp(main, ln_gain, w_spatial, b_spatial_t, out_gain):
    T = main.shape[0]
    uv_block = 4 * D_MLSTM // (2 * D_GMLP)
    return pl.pallas_call(
        _gmlp_kernel,
        grid=(T // TG_GMLP,),
        in_specs=[
            pl.BlockSpec((TG_GMLP, 2 * D_GMLP), lambda i: (i, uv_block)),
            pl.BlockSpec((1, D_GMLP), lambda i: (0, 0)),
            pl.BlockSpec((G_GROUPS, CHUNK, CHUNK), lambda i: (0, 0, 0)),
            pl.BlockSpec((CHUNK, G_GROUPS), lambda i: (0, 0)),
            pl.BlockSpec((1, D_GMLP), lambda i: (0, 0)),
        ],
        out_specs=pl.BlockSpec((TG_GMLP, D_GMLP), lambda i: (i, 0)),
        out_shape=jax.ShapeDtypeStruct((T, D_GMLP), BF16),
        scratch_shapes=[pltpu.VMEM((TG_GMLP, D_GMLP), F32)],
        compiler_params=_cparams(),
        name="gmlp",
    )(main, ln_gain, w_spatial, b_spatial_t, out_gain)


def _outproj_kernel(hm_ref, hg_ref, x_ref, wo_ref, g2_ref, wrt_ref, brt_ref, triu_ref,
                    h_ref, hn_ref, route_ref, cnt_ref, carry_ref):
    i = pl.program_id(0)
    tm = TM_ROUTE

    @pl.when(i == 0)
    def _():
        carry_ref[...] = jnp.zeros_like(carry_ref)

    y = (jnp.dot(hm_ref[...], wo_ref[0:D_MLSTM, :], preferred_element_type=F32)
         + jnp.dot(hg_ref[...], wo_ref[D_MLSTM:D_MODEL, :], preferred_element_type=F32))
    h = x_ref[...] + y
    h_ref[...] = h
    hn = h * lax.rsqrt(jnp.mean(h * h, axis=-1, keepdims=True) + EPS) * g2_ref[...]
    _store_row_tiled(hn_ref, hn)

    lt = lax.dot_general(wrt_ref[...], hn.astype(BF16), (((1,), (1,)), ((), ())),
                         preferred_element_type=F32) + brt_ref[...]
    e_id = lax.broadcasted_iota(jnp.int32, (N_EXPERTS, tm), 0).astype(F32)
    l = lt
    top_v, top_i = [], []
    for _ in range(TOP_K):
        mx = jnp.max(l, axis=0, keepdims=True)
        ix = jnp.min(jnp.where(l == mx, e_id, float(N_EXPERTS)), axis=0, keepdims=True)
        top_v.append(mx)
        top_i.append(ix)
        l = jnp.where(e_id == ix, -jnp.inf, l)
    ex = [jnp.exp(v - top_v[0]) for v in top_v]
    inv = 1.0 / (ex[0] + ex[1] + ex[2] + ex[3])

    onehot = [e_id == ix for ix in top_i]
    mask = (onehot[0].astype(F32) + onehot[1].astype(F32)
            + onehot[2].astype(F32) + onehot[3].astype(F32))
    incl = jnp.dot(mask.astype(BF16), triu_ref[...], preferred_element_type=F32)
    base = carry_ref[:, 0:1] + incl - mask
    ranks = [jnp.sum(jnp.where(onehot[k], base, 0.0), axis=0, keepdims=True)
             for k in range(TOP_K)]
    rows = top_i + ranks + [ex[k] * inv for k in range(TOP_K)]
    route_t = jnp.concatenate(rows + [jnp.zeros((LANES - len(rows), tm), F32)], axis=0)
    route_ref[...] = route_t.T
    carry_ref[...] = carry_ref[...] + incl[:, tm - 1:tm]
    cnt_ref[...] = carry_ref[...]


def _outproj(h_m, h_g, x2, w_out, gain2, w_router_t, b_router_col, triu):
    T = x2.shape[0]
    tm = TM_ROUTE
    return pl.pallas_call(
        _outproj_kernel,
        grid=(T // tm,),
        in_specs=[
            pl.BlockSpec((tm, D_MLSTM), lambda i: (i, 0)),
            pl.BlockSpec((tm, D_GMLP), lambda i: (i, 0)),
            pl.BlockSpec((tm, D_MODEL), lambda i: (i, 0)),
            pl.BlockSpec((D_MODEL, D_MODEL), lambda i: (0, 0)),
            pl.BlockSpec((1, D_MODEL), lambda i: (0, 0)),
            pl.BlockSpec((N_EXPERTS, D_MODEL), lambda i: (0, 0)),
            pl.BlockSpec((N_EXPERTS, 1), lambda i: (0, 0)),
            pl.BlockSpec((tm, tm), lambda i: (0, 0)),
        ],
        out_specs=[
            pl.BlockSpec((tm, D_MODEL), lambda i: (i, 0)),
            pl.BlockSpec((tm * ROW_TILES, LANES), lambda i: (i, 0)),
            pl.BlockSpec((tm, LANES), lambda i: (i, 0)),
            pl.BlockSpec((N_EXPERTS, LANES), lambda i: (0, 0)),
        ],
        out_shape=[
            jax.ShapeDtypeStruct((T, D_MODEL), F32),
            jax.ShapeDtypeStruct((T * ROW_TILES, LANES), F32),
            jax.ShapeDtypeStruct((T, LANES), F32),
            jax.ShapeDtypeStruct((N_EXPERTS, LANES), F32),
        ],
        scratch_shapes=[pltpu.VMEM((N_EXPERTS, LANES), F32)],
        compiler_params=_cparams(),
        name="outproj_router",
    )(h_m, h_g, x2, w_out, gain2, w_router_t, b_router_col, triu)


def _dest_kernel(route_ref, ptab_ref, dest_ref):
    r = route_ref[...]
    tm = r.shape[0]
    lane = lax.broadcasted_iota(jnp.int32, (tm, LANES), 1).astype(F32)
    out = jnp.zeros((tm, LANES), F32)
    for k in range(TOP_K):
        ix = r[:, k:k + 1]
        rank = r[:, TOP_K + k:TOP_K + k + 1]
        start = jnp.dot((lane == ix).astype(F32), ptab_ref[...],
                        precision=lax.Precision.HIGHEST, preferred_element_type=F32)
        out = jnp.where(lane == k, rank + start, out)
    dest_ref[...] = out.astype(jnp.int32)


def _dest(route, pstart):
    T = route.shape[0]
    tm = TM_ROUTE
    return pl.pallas_call(
        _dest_kernel,
        grid=(T // tm,),
        in_specs=[
            pl.BlockSpec((tm, LANES), lambda i: (i, 0)),
            pl.BlockSpec((LANES, LANES), lambda i: (0, 0)),
        ],
        out_specs=pl.BlockSpec((tm, LANES), lambda i: (i, 0)),
        out_shape=jax.ShapeDtypeStruct((T, LANES), jnp.int32),
        compiler_params=_cparams(),
        name="dest",
    )(route, pstart)


def _row_slice(n):
    return pl.ds(pl.multiple_of(n * ROW_TILES, ROW_TILES), ROW_TILES)


def _dispatch_kernel(dest_ref, pend_ref, hn_ref, xs_hbm, zero_ref, src_ref, sem, zsem):
    i = pl.program_id(0)
    base = i * TD_DISPATCH
    blk_rows = BM_FFN * ROW_TILES

    @pl.when(i == 0)
    def _():
        zero_ref[...] = jnp.zeros_like(zero_ref)

        def zero_copy(e):
            start = pl.multiple_of((pend_ref[e] - BM_FFN) * ROW_TILES, blk_rows)
            return pltpu.make_async_copy(zero_ref, xs_hbm.at[pl.ds(start, blk_rows)], zsem)

        def nonempty(e):
            return pend_ref[e] > (pend_ref[e - 1] if e > 0 else 0)

        nb = xs_hbm.shape[0] // blk_rows

        def tail_copy(j):
            return pltpu.make_async_copy(
                zero_ref, xs_hbm.at[pl.ds((nb - 1 - j) * blk_rows, blk_rows)], zsem)

        def unused(j):
            return (nb - 1 - j) * BM_FFN >= pend_ref[N_EXPERTS - 1]

        for e in range(N_EXPERTS):
            @pl.when(nonempty(e))
            def _():
                zero_copy(e).start()

            @pl.when(unused(e))
            def _():
                tail_copy(e).start()
        for e in range(N_EXPERTS):
            @pl.when(nonempty(e))
            def _():
                zero_copy(e).wait()

            @pl.when(unused(e))
            def _():
                tail_copy(e).wait()

    slot = i % 2
    src_ref[slot] = hn_ref[...]

    def row_copy(s, t, k):
        d = dest_ref[(base + t) * TOP_K + k]
        return pltpu.make_async_copy(src_ref.at[s, _row_slice(t)], xs_hbm.at[_row_slice(d)],
                                     sem.at[s])

    def issue(t, carry):
        for k in range(TOP_K):
            row_copy(slot, t, k).start(priority=k % DMA_QUEUES)
        return carry

    def drain(s):
        def body(t, carry):
            for k in range(TOP_K):
                row_copy(s, t, k).wait()
            return carry
        lax.fori_loop(0, TD_DISPATCH, body, 0, unroll=8)

    lax.fori_loop(0, TD_DISPATCH, issue, 0, unroll=8)

    @pl.when(i > 0)
    def _():
        drain(1 - slot)

    @pl.when(i == pl.num_programs(0) - 1)
    def _():
        drain(slot)


def _dispatch(dest_flat, pend, hn_rt, nbuf):
    T = hn_rt.shape[0] // ROW_TILES
    return pl.pallas_call(
        _dispatch_kernel,
        grid_spec=pltpu.PrefetchScalarGridSpec(
            num_scalar_prefetch=2,
            grid=(T // TD_DISPATCH,),
            in_specs=[pl.BlockSpec((TD_DISPATCH * ROW_TILES, LANES), lambda i, d, p: (i, 0))],
            out_specs=pl.BlockSpec(memory_space=pl.ANY),
            scratch_shapes=[
                pltpu.VMEM((BM_FFN * ROW_TILES, LANES), F32),
                pltpu.VMEM((2, TD_DISPATCH * ROW_TILES, LANES), F32),
                pltpu.SemaphoreType.DMA((2,)),
                pltpu.SemaphoreType.DMA,
            ],
        ),
        out_shape=jax.ShapeDtypeStruct((nbuf * ROW_TILES, LANES), F32),
        compiler_params=_cparams(),
        name="dispatch",
    )(dest_flat, pend, hn_rt)


def _ffn_kernel(bstart_ref, bcnt_ref, xs_hbm, wup_ref, bup_ref, wdn_ref, bdn_ref, y_hbm,
                xbuf, ybuf, wup_bf, wdn_bf, in_sem, out_sem):
    e = pl.program_id(0)
    blk_rows = BM_FFN * ROW_TILES
    b0 = bstart_ref[e]
    cnt = bcnt_ref[e]

    def block_rows(j):
        return pl.ds(pl.multiple_of(j * blk_rows, blk_rows), blk_rows)

    def in_copy(j, slot):
        return pltpu.make_async_copy(xs_hbm.at[block_rows(j)], xbuf.at[slot], in_sem.at[slot])

    def out_copy(j, slot):
        return pltpu.make_async_copy(ybuf.at[slot], y_hbm.at[block_rows(j)], out_sem.at[slot])

    @pl.when(cnt > 0)
    def _():
        in_copy(b0, 0).start(priority=1)
        wup_bf[...] = wup_ref[0].astype(BF16)
        wdn_bf[...] = wdn_ref[0].astype(BF16)

        def body(j, carry):
            slot = j % 2
            in_copy(b0 + j, slot).wait()

            @pl.when(j + 1 < cnt)
            def _():
                in_copy(b0 + j + 1, 1 - slot).start(priority=1)

            @pl.when(j >= 2)
            def _():
                out_copy(b0 + j - 2, slot).wait()

            xb = _load_row_tiled(xbuf.at[slot], BM_FFN).astype(BF16)
            hb = jnp.dot(xb, wup_bf[...], preferred_element_type=F32) + bup_ref[0]
            glu = jnp.minimum(hb[:, 0:D_FF], SWIGLU_LIMIT)
            lin = jnp.clip(hb[:, D_FF:2 * D_FF], -SWIGLU_LIMIT, SWIGLU_LIMIT)
            act = glu * _sigmoid(SWIGLU_ALPHA * glu) * (lin + 1.0)
            y = jnp.dot(act.astype(BF16), wdn_bf[...], preferred_element_type=F32) + bdn_ref[0]
            _store_row_tiled(ybuf.at[slot], y)
            out_copy(b0 + j, slot).start(priority=1)
            return carry

        lax.fori_loop(0, cnt, body, 0)

        @pl.when(cnt >= 2)
        def _():
            out_copy(b0 + cnt - 2, cnt % 2).wait()

        out_copy(b0 + cnt - 1, (cnt - 1) % 2).wait()

    @pl.when(e == N_EXPERTS - 1)
    def _():
        nb = y_hbm.shape[0] // blk_rows
        n_valid = b0 + cnt
        ybuf[0] = jnp.zeros((blk_rows, LANES), F32)
        for j in range(N_EXPERTS):
            @pl.when(nb - 1 - j >= n_valid)
            def _():
                out_copy(nb - 1 - j, 0).start()
        for j in range(N_EXPERTS):
            @pl.when(nb - 1 - j >= n_valid)
            def _():
                out_copy(nb - 1 - j, 0).wait()


def _ffn(bstart, bcnt, xs_rt, w_up, b_up3, w_down, b_down3):
    blk_rows = BM_FFN * ROW_TILES
    wspec = lambda shape: pl.BlockSpec(shape, lambda e, bs, bc: (e, 0, 0))
    return pl.pallas_call(
        _ffn_kernel,
        grid_spec=pltpu.PrefetchScalarGridSpec(
            num_scalar_prefetch=2,
            grid=(N_EXPERTS,),
            in_specs=[
                pl.BlockSpec(memory_space=pl.ANY),
                wspec((1, D_MODEL, 2 * D_FF)),
                wspec((1, 1, 2 * D_FF)),
                wspec((1, D_FF, D_MODEL)),
                wspec((1, 1, D_MODEL)),
            ],
            out_specs=pl.BlockSpec(memory_space=pl.ANY),
            scratch_shapes=[
                pltpu.VMEM((2, blk_rows, LANES), F32),
                pltpu.VMEM((2, blk_rows, LANES), F32),
                pltpu.VMEM((D_MODEL, 2 * D_FF), BF16),
                pltpu.VMEM((D_FF, D_MODEL), BF16),
                pltpu.SemaphoreType.DMA((2,)),
                pltpu.SemaphoreType.DMA((2,)),
            ],
        ),
        out_shape=jax.ShapeDtypeStruct(xs_rt.shape, F32),
        compiler_params=_cparams(),
        name="expert_ffn",
    )(bstart, bcnt, xs_rt, w_up, b_up3, w_down, b_down3)


def _combine_kernel(dest_ref, route_ref, h_ref, gain_ref, yb_hbm, out_ref, buf_ref, sem):
    tm = TM_COMBINE
    i = pl.program_id(0)

    def row_copy(step, t, k):
        d = dest_ref[(step * tm + t) * TOP_K + k]
        slot = step % 2
        return pltpu.make_async_copy(
            yb_hbm.at[_row_slice(d)], buf_ref.at[slot, _row_slice(k * tm + t)], sem.at[slot])

    def issue(step):
        def body(t, carry):
            for k in range(TOP_K):
                row_copy(step, t, k).start(priority=k % DMA_QUEUES)
            return carry
        lax.fori_loop(0, tm, body, 0, unroll=8)

    def drain(step):
        def body(t, carry):
            for k in range(TOP_K):
                row_copy(step, t, k).wait()
            return carry
        lax.fori_loop(0, tm, body, 0, unroll=8)

    @pl.when(i == 0)
    def _():
        issue(i)

    @pl.when(i + 1 < pl.num_programs(0))
    def _():
        issue(i + 1)

    drain(i)

    r = route_ref[...]
    acc = h_ref[...]
    cur = buf_ref.at[i % 2]
    for k in range(TOP_K):
        acc = acc + r[:, 2 * TOP_K + k:2 * TOP_K + k + 1] * _load_row_tiled(cur, tm, k * tm)
    out = acc * lax.rsqrt(jnp.mean(acc * acc, axis=-1, keepdims=True) + EPS) * gain_ref[...]
    out_ref[...] = out


def _combine(dest_flat, route, h, gain, yb_rt):
    T = h.shape[0]
    tm = TM_COMBINE
    return pl.pallas_call(
        _combine_kernel,
        grid_spec=pltpu.PrefetchScalarGridSpec(
            num_scalar_prefetch=1,
            grid=(T // tm,),
            in_specs=[
                pl.BlockSpec((tm, LANES), lambda i, d: (i, 0)),
                pl.BlockSpec((tm, D_MODEL), lambda i, d: (i, 0)),
                pl.BlockSpec((1, D_MODEL), lambda i, d: (0, 0)),
                pl.BlockSpec(memory_space=pl.ANY),
            ],
            out_specs=pl.BlockSpec((tm, D_MODEL), lambda i, d: (i, 0)),
            scratch_shapes=[
                pltpu.VMEM((2, TOP_K * tm * ROW_TILES, LANES), F32),
                pltpu.SemaphoreType.DMA((2,)),
            ],
        ),
        out_shape=jax.ShapeDtypeStruct((T, D_MODEL), F32),
        compiler_params=_cparams(),
        name="combine",
    )(dest_flat, route, h, gain, yb_rt)


def _layer(x2, B, S, norm1_gain, w_in, conv_qk, b_igate, b_fgate, mlstm_norm_gain, gmlp_ln_gain,
           w_spatial, b_spatial, gmlp_out_gain, w_out, norm2_gain, w_router, b_router,
           w_up, b_up, w_down, b_down):
    T = B * S
    n_gate = 2 * M_HEADS
    g0 = 4 * D_MLSTM
    w_main = jnp.concatenate([w_in[:, :g0], w_in[:, g0 + n_gate:]], axis=1).astype(BF16)
    w_gate = jnp.pad(w_in[:, g0:g0 + n_gate], ((0, 0), (0, LANES - n_gate))).astype(BF16)
    gbias = jnp.pad(jnp.concatenate([b_igate, b_fgate]), (0, LANES - n_gate)).reshape(1, LANES)

    main, gates = _inproj(x2, norm1_gain.reshape(1, D_MODEL), w_main, w_gate)
    h_m = _mlstm(main.reshape(B, S, D_MAIN), gates.reshape(B, S, LANES), conv_qk, gbias,
                 mlstm_norm_gain.reshape(1, D_MLSTM)).reshape(T, D_MLSTM)
    h_g = _gmlp(main, gmlp_ln_gain.reshape(1, D_GMLP), w_spatial, b_spatial.T,
                gmlp_out_gain.reshape(1, D_GMLP))

    triu = jnp.triu(jnp.ones((TM_ROUTE, TM_ROUTE), BF16))
    h, hn_rt, route, cnt = _outproj(h_m, h_g, x2, w_out.astype(BF16),
                                    norm2_gain.reshape(1, D_MODEL), w_router.T.astype(BF16),
                                    b_router.reshape(N_EXPERTS, 1), triu)

    counts = cnt[:, 0].astype(jnp.int32)
    padded = (counts + BM_FFN - 1) // BM_FFN * BM_FFN
    pend = jnp.cumsum(padded)
    pstart = pend - padded
    nb = (T * TOP_K) // BM_FFN + N_EXPERTS
    pstart_tab = jnp.broadcast_to(
        jnp.pad(pstart.astype(F32), (0, LANES - N_EXPERTS))[:, None], (LANES, LANES))

    dest = _dest(route, pstart_tab)
    dest_flat = dest[:, :TOP_K].reshape(T * TOP_K)

    xs_rt = _dispatch(dest_flat, pend.astype(jnp.int32), hn_rt, nb * BM_FFN)
    yb_rt = _ffn((pstart // BM_FFN).astype(jnp.int32), (padded // BM_FFN).astype(jnp.int32), xs_rt,
                 w_up, b_up.reshape(N_EXPERTS, 1, 2 * D_FF), w_down,
                 b_down.reshape(N_EXPERTS, 1, D_MODEL))
    return dest_flat, route, h, yb_rt


def kernel(x, norm1_gain, w_in, conv_qk, b_igate, b_fgate, mlstm_norm_gain, gmlp_ln_gain,
           w_spatial, b_spatial, gmlp_out_gain, w_out, norm2_gain, w_router, b_router,
           w_up, b_up, w_down, b_down, final_gain):
    B, S, D = x.shape
    depth = norm1_gain.shape[0]
    assert depth == 1 and D == D_MODEL and S % CHUNK == 0
    x2 = x.reshape(B * S, D)
    l = 0
    dest_flat, route, h, yb_rt = _layer(
        x2, B, S, norm1_gain[l], w_in[l], conv_qk[l], b_igate[l], b_fgate[l], mlstm_norm_gain[l],
        gmlp_ln_gain[l], w_spatial[l], b_spatial[l], gmlp_out_gain[l], w_out[l], norm2_gain[l],
        w_router[l], b_router[l], w_up[l], b_up[l], w_down[l], b_down[l])
    out = _combine(dest_flat, route, h, final_gain.reshape(1, D_MODEL), yb_rt)
    return out.reshape(B, S, D)
```

```python
import functools

import jax
import jax.numpy as jnp
from jax import lax
from jax.experimental import pallas as pl
from jax.experimental.pallas import tpu as pltpu

F32 = jnp.float32
BF16 = jnp.bfloat16

D_MODEL = 1024
M_HEADS = 4
M_HEAD_DIM = 128
D_MLSTM = M_HEADS * M_HEAD_DIM
D_GMLP = D_MODEL - D_MLSTM
G_GROUPS = 4
G_DIM = D_GMLP // G_GROUPS
CHUNK = 128
CONV_K = 4
N_EXPERTS = 32
TOP_K = 4
D_FF = D_MODEL
SWIGLU_LIMIT = 7.0
SWIGLU_ALPHA = 1.702
EPS = 1e-6

LANES = 128
SUBLANES = 8
ROW_TILES = D_MODEL // LANES
assert ROW_TILES == SUBLANES
D_MAIN = 4 * D_MLSTM + 2 * D_GMLP
TM_PROJ = 512
TG_GMLP = 512
TM_ROUTE = 512
TD_DISPATCH = 512
TM_COMBINE = 512
BM_FFN = 256
DMA_QUEUES = 2
VMEM_LIMIT = 56 * 1024 * 1024


def _cparams(n_axes=1):
    return pltpu.CompilerParams(
        dimension_semantics=("arbitrary",) * n_axes, vmem_limit_bytes=VMEM_LIMIT)


def _sigmoid(x):
    return 1.0 / (1.0 + jnp.exp(-x))


def _gelu_tanh(x):
    c = 0.7978845608028654
    return x * (0.5 * (1.0 + jnp.tanh(c * (x + 0.044715 * (x * x * x)))))


def _log_sigmoid(x):
    return jnp.minimum(x, 0.0) - jnp.log1p(jnp.exp(-jnp.abs(x)))


def _store_row_tiled(ref, x, row0=0):
    n_rows = x.shape[0]
    for g in range(n_rows // SUBLANES):
        for j in range(ROW_TILES):
            ref[pl.ds((row0 + g * SUBLANES) * ROW_TILES + j, SUBLANES, stride=ROW_TILES), :] = (
                x[g * SUBLANES:(g + 1) * SUBLANES, j * LANES:(j + 1) * LANES])


def _load_row_tiled(ref, n_rows, row0=0):
    groups = []
    for g in range(n_rows // SUBLANES):
        tiles = [ref[pl.ds((row0 + g * SUBLANES) * ROW_TILES + j, SUBLANES, stride=ROW_TILES), :]
                 for j in range(ROW_TILES)]
        groups.append(jnp.concatenate(tiles, axis=1))
    return jnp.concatenate(groups, axis=0)


def _inproj_kernel(x_ref, gain_ref, wm_ref, wg_ref, main_ref, gate_ref):
    x = x_ref[...]
    xn = x * lax.rsqrt(jnp.mean(x * x, axis=-1, keepdims=True) + EPS) * gain_ref[...]
    xb = xn.astype(BF16)
    main_ref[...] = jnp.dot(xb, wm_ref[...], preferred_element_type=F32).astype(BF16)
    gate_ref[...] = jnp.dot(xb, wg_ref[...], preferred_element_type=F32)


def _inproj(x2, gain, w_main, w_gate):
    T = x2.shape[0]
    return pl.pallas_call(
        _inproj_kernel,
        grid=(T // TM_PROJ,),
        in_specs=[
            pl.BlockSpec((TM_PROJ, D_MODEL), lambda i: (i, 0)),
            pl.BlockSpec((1, D_MODEL), lambda i: (0, 0)),
            pl.BlockSpec((D_MODEL, D_MAIN), lambda i: (0, 0)),
            pl.BlockSpec((D_MODEL, LANES), lambda i: (0, 0)),
        ],
        out_specs=[
            pl.BlockSpec((TM_PROJ, D_MAIN), lambda i: (i, 0)),
            pl.BlockSpec((TM_PROJ, LANES), lambda i: (i, 0)),
        ],
        out_shape=[
            jax.ShapeDtypeStruct((T, D_MAIN), BF16),
            jax.ShapeDtypeStruct((T, LANES), F32),
        ],
        compiler_params=_cparams(),
        name="inproj",
    )(x2, gain, w_main, w_gate)


def _mlstm_kernel(qkvo_ref, gates_ref, conv_ref, gbias_ref, ngain_ref, out_ref,
                  xp_ref, ct_ref, n_ref, m_ref, *, batch):
    c = pl.program_id(0)
    L, DH = CHUNK, M_HEAD_DIM

    @pl.when(c == 0)
    def _():
        xp_ref[:, 0:SUBLANES, :] = jnp.zeros((batch, SUBLANES, 2 * D_MLSTM), F32)
        ct_ref[...] = jnp.zeros_like(ct_ref)
        n_ref[...] = jnp.zeros_like(n_ref)
        m_ref[...] = jnp.zeros_like(m_ref)

    row = lax.broadcasted_iota(jnp.int32, (L, L), 0)
    col = lax.broadcasted_iota(jnp.int32, (L, L), 1)
    tril = row >= col
    tril_f = tril.astype(F32)
    chains = [(b, h) for b in range(batch) for h in range(M_HEADS)]


    qk, gc, ball, gc_t, ball_t = [], [], [], [], []
    for b in range(batch):
        xp_ref[b, SUBLANES:SUBLANES + L, :] = qkvo_ref[b, :, 0:2 * D_MLSTM].astype(F32)
        off = SUBLANES - (CONV_K - 1)
        acc = xp_ref[b, off:off + L, :] * conv_ref[0:1, :]
        for j in range(1, CONV_K):
            acc = acc + xp_ref[b, off + j:off + j + L, :] * conv_ref[j:j + 1, :]
        xp_ref[b, 0:SUBLANES, :] = xp_ref[b, L:L + SUBLANES, :]
        qk.append(acc * _sigmoid(acc))
        g = gates_ref[b] + gbias_ref[...]
        gc.append(g)
        ball.append(jnp.dot(tril_f, _log_sigmoid(g), precision=lax.Precision.HIGHEST,
                            preferred_element_type=F32))
    for b in range(batch):
        gc_t.append(gc[b].T)
        ball_t.append(ball[b].T)

    st = []
    for b, h in chains:
        s = b * M_HEADS + h
        q = qk[b][:, h * DH:(h + 1) * DH]
        k = qk[b][:, D_MLSTM + h * DH:D_MLSTM + (h + 1) * DH] * (DH ** -0.5)
        d = dict(
            s=s, b=b, h=h, q=q, k=k, qb=q.astype(BF16), kb=k.astype(BF16),
            v=qkvo_ref[b, :, 2 * D_MLSTM + h * DH:2 * D_MLSTM + (h + 1) * DH],
            b_col=ball[b][:, M_HEADS + h:M_HEADS + h + 1],
            b_row=ball_t[b][M_HEADS + h:M_HEADS + h + 1, :],
            li_col=gc[b][:, h:h + 1],
            li_row=gc_t[b][h:h + 1, :],
            m_prev=m_ref[s][0:1, 0:1],
            ct_prev=ct_ref[s],
            n_prev=n_ref[s],
        )
        d["b_last"] = d["b_col"][L - 1:L, :]
        st.append(d)
    for d in st:
        d["qkt"] = lax.dot_general(d["qb"], d["kb"], (((1,), (1,)), ((), ())),
                                   preferred_element_type=F32)
        d["q_ct"] = jnp.dot(d["qb"], d["ct_prev"].astype(BF16), preferred_element_type=F32)
    for d in st:
        d["dmat"] = jnp.where(tril, d["b_col"] - d["b_row"] + d["li_row"], -jnp.inf)
        d["a_inter"] = d["b_col"] + d["m_prev"]
    for d in st:
        d["m_t"] = jnp.maximum(d["a_inter"], jnp.max(d["dmat"], axis=-1, keepdims=True))

    for d in st:
        d["w"] = jnp.exp(d["dmat"] - d["m_t"]) * d["qkt"]
        d["inter"] = jnp.exp(d["a_inter"] - d["m_t"])
    for d in st:
        d["num"] = (jnp.dot(d["w"].astype(BF16), d["v"], preferred_element_type=F32)
                    + d["inter"] * d["q_ct"])
        d["den"] = (jnp.sum(d["w"], axis=-1, keepdims=True)
                    + d["inter"] * jnp.sum(d["q"] * d["n_prev"], axis=-1, keepdims=True))
    for d in st:
        d["hh"] = d["num"] * (1.0 / jnp.maximum(jnp.abs(d["den"]), jnp.exp(-d["m_t"])))

    for d in st:
        d["mu"] = jnp.mean(d["hh"], axis=-1, keepdims=True)
    for d in st:
        d["xc"] = d["hh"] - d["mu"]
        d["var"] = jnp.mean(d["xc"] * d["xc"], axis=-1, keepdims=True)
    for d in st:
        b, h = d["b"], d["h"]
        y = d["xc"] * lax.rsqrt(d["var"] + EPS) * ngain_ref[:, h * DH:(h + 1) * DH]
        o = qkvo_ref[b, :, 3 * D_MLSTM + h * DH:3 * D_MLSTM + (h + 1) * DH].astype(F32)
        out_ref[b, :, h * DH:(h + 1) * DH] = (_sigmoid(o) * y).astype(out_ref.dtype)

    for d in st:
        d["g_col"] = d["b_last"] - d["b_col"] + d["li_col"]
        d["m_loc"] = jnp.max(d["g_col"], axis=0, keepdims=True)
    for d in st:
        d["w_col"] = jnp.exp(d["g_col"] - d["m_loc"])
        d["k_t"] = d["k"].T.astype(BF16)
    for d in st:
        wv = (d["w_col"] * d["v"].astype(F32)).astype(BF16)
        d["ct_loc"] = jnp.dot(d["k_t"], wv, preferred_element_type=F32)
        d["n_loc"] = jnp.sum(d["w_col"] * d["k"], axis=0, keepdims=True)
    for d in st:
        s = d["s"]
        m_new = jnp.maximum(d["b_last"] + d["m_prev"], d["m_loc"])
        a = jnp.exp(d["b_last"] + d["m_prev"] - m_new)
        cc = jnp.exp(d["m_loc"] - m_new)
        ct_ref[s] = a * d["ct_prev"] + cc * d["ct_loc"]
        n_ref[s] = a * d["n_prev"] + cc * d["n_loc"]
        m_ref[s] = jnp.broadcast_to(m_new, (SUBLANES, LANES))


def _mlstm(main3, gates3, conv_qk, gbias, ngain):
    B, S, _ = main3.shape
    nc = S // CHUNK
    return pl.pallas_call(
        functools.partial(_mlstm_kernel, batch=B),
        grid=(nc,),
        in_specs=[
            pl.BlockSpec((B, CHUNK, 4 * D_MLSTM), lambda c: (0, c, 0)),
            pl.BlockSpec((B, CHUNK, LANES), lambda c: (0, c, 0)),
            pl.BlockSpec((CONV_K, 2 * D_MLSTM), lambda c: (0, 0)),
            pl.BlockSpec((1, LANES), lambda c: (0, 0)),
            pl.BlockSpec((1, D_MLSTM), lambda c: (0, 0)),
        ],
        out_specs=pl.BlockSpec((B, CHUNK, D_MLSTM), lambda c: (0, c, 0)),
        out_shape=jax.ShapeDtypeStruct((B, S, D_MLSTM), BF16),
        scratch_shapes=[
            pltpu.VMEM((B, CHUNK + SUBLANES, 2 * D_MLSTM), F32),
            pltpu.VMEM((B * M_HEADS, M_HEAD_DIM, M_HEAD_DIM), F32),
            pltpu.VMEM((B * M_HEADS, 1, M_HEAD_DIM), F32),
            pltpu.VMEM((B * M_HEADS, SUBLANES, LANES), F32),
        ],
        compiler_params=_cparams(),
        name="mlstm",
    )(main3, gates3, conv_qk, gbias, ngain)


def _gmlp_kernel(uv_ref, lng_ref, ws_ref, bs_ref, og_ref, out_ref, prod_ref):
    L = CHUNK
    u = _gelu_tanh(uv_ref[:, 0:D_GMLP].astype(F32))
    vg = _gelu_tanh(uv_ref[:, D_GMLP:2 * D_GMLP].astype(F32))
    mu = jnp.mean(vg, axis=-1, keepdims=True)
    xc = vg - mu
    vg = xc * lax.rsqrt(jnp.mean(xc * xc, axis=-1, keepdims=True) + EPS) * lng_ref[...]
    vb = vg.astype(BF16)
    row = lax.broadcasted_iota(jnp.int32, (L, L), 0)
    col = lax.broadcasted_iota(jnp.int32, (L, L), 1)
    tril = row >= col
    for g in range(G_GROUPS):
        wg = jnp.where(tril, ws_ref[g], 0.0).astype(BF16)
        b_col = bs_ref[:, g:g + 1]
        for j in range(TG_GMLP // L):
            mixed = jnp.dot(wg, vb[j * L:(j + 1) * L, g * G_DIM:(g + 1) * G_DIM],
                            preferred_element_type=F32) + b_col
            prod_ref[j * L:(j + 1) * L, g * G_DIM:(g + 1) * G_DIM] = (
                u[j * L:(j + 1) * L, g * G_DIM:(g + 1) * G_DIM] * mixed)
    p = prod_ref[...]
    y = p * lax.rsqrt(jnp.mean(p * p, axis=-1, keepdims=True) + EPS) * og_ref[...]
    out_ref[...] = y.astype(out_ref.dtype)


def _gmlp(main, ln_gain, w_spatial, b_spatial_t, out_gain):
    T = main.shape[0]
    uv_block = 4 * D_MLSTM // (2 * D_GMLP)
    return pl.pallas_call(
        _gmlp_kernel,
        grid=(T // TG_GMLP,),
        in_specs=[
            pl.BlockSpec((TG_GMLP, 2 * D_GMLP), lambda i: (i, uv_block)),
            pl.BlockSpec((1, D_GMLP), lambda i: (0, 0)),
            pl.BlockSpec((G_GROUPS, CHUNK, CHUNK), lambda i: (0, 0, 0)),
            pl.BlockSpec((CHUNK, G_GROUPS), lambda i: (0, 0)),
            pl.BlockSpec((1, D_GMLP), lambda i: (0, 0)),
        ],
        out_specs=pl.BlockSpec((TG_GMLP, D_GMLP), lambda i: (i, 0)),
        out_shape=jax.ShapeDtypeStruct((T, D_GMLP), BF16),
        scratch_shapes=[pltpu.VMEM((TG_GMLP, D_GMLP), F32)],
        compiler_params=_cparams(),
        name="gmlp",
    )(main, ln_gain, w_spatial, b_spatial_t, out_gain)


ROUTE_ROWS = 16


def _outproj_kernel(hm_ref, hg_ref, x_ref, wo_ref, g2_ref, wrt_ref, brt_ref, triu_ref,
                    h_ref, hn_ref, route_ref, route_t_ref, cnt_ref, carry_ref):
    i = pl.program_id(0)
    tm = TM_ROUTE

    @pl.when(i == 0)
    def _():
        carry_ref[...] = jnp.zeros_like(carry_ref)

    y = (jnp.dot(hm_ref[...], wo_ref[0:D_MLSTM, :], preferred_element_type=F32)
         + jnp.dot(hg_ref[...], wo_ref[D_MLSTM:D_MODEL, :], preferred_element_type=F32))
    h = x_ref[...] + y
    h_ref[...] = h
    hn = h * lax.rsqrt(jnp.mean(h * h, axis=-1, keepdims=True) + EPS) * g2_ref[...]
    _store_row_tiled(hn_ref, hn)

    lt = lax.dot_general(wrt_ref[...], hn.astype(BF16), (((1,), (1,)), ((), ())),
                         preferred_element_type=F32) + brt_ref[...]
    e_id = lax.broadcasted_iota(jnp.int32, (N_EXPERTS, tm), 0).astype(F32)
    l = lt
    top_v, top_i = [], []
    for _ in range(TOP_K):
        mx = jnp.max(l, axis=0, keepdims=True)
        ix = jnp.min(jnp.where(l == mx, e_id, float(N_EXPERTS)), axis=0, keepdims=True)
        top_v.append(mx)
        top_i.append(ix)
        l = jnp.where(e_id == ix, -jnp.inf, l)
    ex = [jnp.exp(v - top_v[0]) for v in top_v]
    inv = 1.0 / (ex[0] + ex[1] + ex[2] + ex[3])

    onehot = [e_id == ix for ix in top_i]
    mask = (onehot[0].astype(F32) + onehot[1].astype(F32)
            + onehot[2].astype(F32) + onehot[3].astype(F32))
    incl = jnp.dot(mask.astype(BF16), triu_ref[...], preferred_element_type=F32)
    base = carry_ref[:, 0:1] + incl - mask
    ranks = [jnp.sum(jnp.where(onehot[k], base, 0.0), axis=0, keepdims=True)
             for k in range(TOP_K)]
    rows = top_i + ranks + [ex[k] * inv for k in range(TOP_K)]
    route_t = jnp.concatenate(rows + [jnp.zeros((LANES - len(rows), tm), F32)], axis=0)
    route_ref[...] = route_t.T
    route_t_ref[...] = route_t[0:ROUTE_ROWS, :]
    carry_ref[...] = carry_ref[...] + incl[:, tm - 1:tm]
    cnt_ref[...] = carry_ref[...]


def _outproj(h_m, h_g, x2, w_out, gain2, w_router_t, b_router_col, triu):
    T = x2.shape[0]
    tm = TM_ROUTE
    return pl.pallas_call(
        _outproj_kernel,
        grid=(T // tm,),
        in_specs=[
            pl.BlockSpec((tm, D_MLSTM), lambda i: (i, 0)),
            pl.BlockSpec((tm, D_GMLP), lambda i: (i, 0)),
            pl.BlockSpec((tm, D_MODEL), lambda i: (i, 0)),
            pl.BlockSpec((D_MODEL, D_MODEL), lambda i: (0, 0)),
            pl.BlockSpec((1, D_MODEL), lambda i: (0, 0)),
            pl.BlockSpec((N_EXPERTS, D_MODEL), lambda i: (0, 0)),
            pl.BlockSpec((N_EXPERTS, 1), lambda i: (0, 0)),
            pl.BlockSpec((tm, tm), lambda i: (0, 0)),
        ],
        out_specs=[
            pl.BlockSpec((tm, D_MODEL), lambda i: (i, 0)),
            pl.BlockSpec((tm * ROW_TILES, LANES), lambda i: (i, 0)),
            pl.BlockSpec((tm, LANES), lambda i: (i, 0)),
            pl.BlockSpec((ROUTE_ROWS, tm), lambda i: (0, i)),
            pl.BlockSpec((N_EXPERTS, LANES), lambda i: (0, 0)),
        ],
        out_shape=[
            jax.ShapeDtypeStruct((T, D_MODEL), F32),
            jax.ShapeDtypeStruct((T * ROW_TILES, LANES), F32),
            jax.ShapeDtypeStruct((T, LANES), F32),
            jax.ShapeDtypeStruct((ROUTE_ROWS, T), F32),
            jax.ShapeDtypeStruct((N_EXPERTS, LANES), F32),
        ],
        scratch_shapes=[pltpu.VMEM((N_EXPERTS, LANES), F32)],
        compiler_params=_cparams(),
        name="outproj_router",
    )(h_m, h_g, x2, w_out, gain2, w_router_t, b_router_col, triu)


TM_DEST = 2048


def _dest_kernel(route_t_ref, pstart_ref, dest_ref):
    r = route_t_ref[...]
    tm = r.shape[1]
    e_id = lax.broadcasted_iota(jnp.int32, (N_EXPERTS, tm), 0).astype(F32)
    ps = pstart_ref[...]
    rows = []
    for k in range(TOP_K):
        start = jnp.sum(jnp.where(e_id == r[k:k + 1, :], ps, 0.0), axis=0, keepdims=True)
        rows.append(r[TOP_K + k:TOP_K + k + 1, :] + start)
    rows.append(jnp.zeros((SUBLANES - TOP_K, tm), F32))
    dest_ref[...] = jnp.concatenate(rows, axis=0).astype(jnp.int32)


def _dest(route_t, pstart_col):
    T = route_t.shape[1]
    tm = TM_DEST
    return pl.pallas_call(
        _dest_kernel,
        grid=(T // tm,),
        in_specs=[
            pl.BlockSpec((ROUTE_ROWS, tm), lambda i: (0, i)),
            pl.BlockSpec((N_EXPERTS, 1), lambda i: (0, 0)),
        ],
        out_specs=pl.BlockSpec((SUBLANES, tm), lambda i: (0, i)),
        out_shape=jax.ShapeDtypeStruct((SUBLANES, T), jnp.int32),
        compiler_params=_cparams(),
        name="dest",
    )(route_t, pstart_col)


def _row_slice(n):
    return pl.ds(pl.multiple_of(n * ROW_TILES, ROW_TILES), ROW_TILES)


def _dispatch_kernel(dest_ref, pend_ref, hn_ref, xs_hbm, zero_ref, src_ref, sem, zsem):
    i = pl.program_id(0)
    base = i * TD_DISPATCH
    blk_rows = BM_FFN * ROW_TILES

    @pl.when(i == 0)
    def _():
        zero_ref[...] = jnp.zeros_like(zero_ref)

        def zero_copy(e):
            start = pl.multiple_of((pend_ref[e] - BM_FFN) * ROW_TILES, blk_rows)
            return pltpu.make_async_copy(zero_ref, xs_hbm.at[pl.ds(start, blk_rows)], zsem)

        def nonempty(e):
            return pend_ref[e] > (pend_ref[e - 1] if e > 0 else 0)

        nb = xs_hbm.shape[0] // blk_rows

        def tail_copy(j):
            return pltpu.make_async_copy(
                zero_ref, xs_hbm.at[pl.ds((nb - 1 - j) * blk_rows, blk_rows)], zsem)

        def unused(j):
            return (nb - 1 - j) * BM_FFN >= pend_ref[N_EXPERTS - 1]

        for e in range(N_EXPERTS):
            @pl.when(nonempty(e))
            def _():
                zero_copy(e).start()

            @pl.when(unused(e))
            def _():
                tail_copy(e).start()
        for e in range(N_EXPERTS):
            @pl.when(nonempty(e))
            def _():
                zero_copy(e).wait()

            @pl.when(unused(e))
            def _():
                tail_copy(e).wait()

    slot = i % 2
    src_ref[slot] = hn_ref[...]

    n_tok = dest_ref.shape[0] // TOP_K

    def row_copy(s, t, k):
        d = dest_ref[k * n_tok + base + t]
        return pltpu.make_async_copy(src_ref.at[s, _row_slice(t)], xs_hbm.at[_row_slice(d)],
                                     sem.at[s])

    def issue(t, carry):
        for k in range(TOP_K):
            row_copy(slot, t, k).start(priority=k % DMA_QUEUES)
        return carry

    def drain(s):
        def body(t, carry):
            for k in range(TOP_K):
                row_copy(s, t, k).wait()
            return carry
        lax.fori_loop(0, TD_DISPATCH, body, 0, unroll=8)

    lax.fori_loop(0, TD_DISPATCH, issue, 0, unroll=8)

    @pl.when(i > 0)
    def _():
        drain(1 - slot)

    @pl.when(i == pl.num_programs(0) - 1)
    def _():
        drain(slot)


def _dispatch(dest_flat, pend, hn_rt, nbuf):
    T = hn_rt.shape[0] // ROW_TILES
    return pl.pallas_call(
        _dispatch_kernel,
        grid_spec=pltpu.PrefetchScalarGridSpec(
            num_scalar_prefetch=2,
            grid=(T // TD_DISPATCH,),
            in_specs=[pl.BlockSpec((TD_DISPATCH * ROW_TILES, LANES), lambda i, d, p: (i, 0))],
            out_specs=pl.BlockSpec(memory_space=pl.ANY),
            scratch_shapes=[
                pltpu.VMEM((BM_FFN * ROW_TILES, LANES), F32),
                pltpu.VMEM((2, TD_DISPATCH * ROW_TILES, LANES), F32),
                pltpu.SemaphoreType.DMA((2,)),
                pltpu.SemaphoreType.DMA,
            ],
        ),
        out_shape=jax.ShapeDtypeStruct((nbuf * ROW_TILES, LANES), F32),
        compiler_params=_cparams(),
        name="dispatch",
    )(dest_flat, pend, hn_rt)


def _ffn_kernel(bstart_ref, bcnt_ref, xs_hbm, wup_ref, bup_ref, wdn_ref, bdn_ref, y_hbm,
                xbuf, ybuf, wup_bf, wdn_bf, in_sem, out_sem):
    e = pl.program_id(0)
    blk_rows = BM_FFN * ROW_TILES
    b0 = bstart_ref[e]
    cnt = bcnt_ref[e]

    def block_rows(j):
        return pl.ds(pl.multiple_of(j * blk_rows, blk_rows), blk_rows)

    def in_copy(j, slot):
        return pltpu.make_async_copy(xs_hbm.at[block_rows(j)], xbuf.at[slot], in_sem.at[slot])

    def out_copy(j, slot):
        return pltpu.make_async_copy(ybuf.at[slot], y_hbm.at[block_rows(j)], out_sem.at[slot])

    @pl.when(cnt > 0)
    def _():
        in_copy(b0, 0).start(priority=1)
        wup_bf[...] = wup_ref[0].astype(BF16)
        wdn_bf[...] = wdn_ref[0].astype(BF16)

        def body(j, carry):
            slot = j % 2
            in_copy(b0 + j, slot).wait()

            @pl.when(j + 1 < cnt)
            def _():
                in_copy(b0 + j + 1, 1 - slot).start(priority=1)

            @pl.when(j >= 2)
            def _():
                out_copy(b0 + j - 2, slot).wait()

            xb = _load_row_tiled(xbuf.at[slot], BM_FFN).astype(BF16)
            hb = jnp.dot(xb, wup_bf[...], preferred_element_type=F32) + bup_ref[0]
            glu = jnp.minimum(hb[:, 0:D_FF], SWIGLU_LIMIT)
            lin = jnp.clip(hb[:, D_FF:2 * D_FF], -SWIGLU_LIMIT, SWIGLU_LIMIT)
            act = glu * _sigmoid(SWIGLU_ALPHA * glu) * (lin + 1.0)
            y = jnp.dot(act.astype(BF16), wdn_bf[...], preferred_element_type=F32) + bdn_ref[0]
            _store_row_tiled(ybuf.at[slot], y)
            out_copy(b0 + j, slot).start(priority=1)
            return carry

        lax.fori_loop(0, cnt, body, 0)

        @pl.when(cnt >= 2)
        def _():
            out_copy(b0 + cnt - 2, cnt % 2).wait()

        out_copy(b0 + cnt - 1, (cnt - 1) % 2).wait()

    @pl.when(e == N_EXPERTS - 1)
    def _():
        nb = y_hbm.shape[0] // blk_rows
        n_valid = b0 + cnt
        ybuf[0] = jnp.zeros((blk_rows, LANES), F32)
        for j in range(N_EXPERTS):
            @pl.when(nb - 1 - j >= n_valid)
            def _():
                out_copy(nb - 1 - j, 0).start()
        for j in range(N_EXPERTS):
            @pl.when(nb - 1 - j >= n_valid)
            def _():
                out_copy(nb - 1 - j, 0).wait()


def _ffn(bstart, bcnt, xs_rt, w_up, b_up3, w_down, b_down3):
    blk_rows = BM_FFN * ROW_TILES
    wspec = lambda shape: pl.BlockSpec(shape, lambda e, bs, bc: (e, 0, 0))
    return pl.pallas_call(
        _ffn_kernel,
        grid_spec=pltpu.PrefetchScalarGridSpec(
            num_scalar_prefetch=2,
            grid=(N_EXPERTS,),
            in_specs=[
                pl.BlockSpec(memory_space=pl.ANY),
                wspec((1, D_MODEL, 2 * D_FF)),
                wspec((1, 1, 2 * D_FF)),
                wspec((1, D_FF, D_MODEL)),
                wspec((1, 1, D_MODEL)),
            ],
            out_specs=pl.BlockSpec(memory_space=pl.ANY),
            scratch_shapes=[
                pltpu.VMEM((2, blk_rows, LANES), F32),
                pltpu.VMEM((2, blk_rows, LANES), F32),
                pltpu.VMEM((D_MODEL, 2 * D_FF), BF16),
                pltpu.VMEM((D_FF, D_MODEL), BF16),
                pltpu.SemaphoreType.DMA((2,)),
                pltpu.SemaphoreType.DMA((2,)),
            ],
        ),
        out_shape=jax.ShapeDtypeStruct(xs_rt.shape, F32),
        compiler_params=_cparams(),
        name="expert_ffn",
    )(bstart, bcnt, xs_rt, w_up, b_up3, w_down, b_down3)


def _combine_kernel(dest_ref, route_ref, h_ref, gain_ref, yb_hbm, out_ref, buf_ref, sem):
    tm = TM_COMBINE
    i = pl.program_id(0)

    n_tok = dest_ref.shape[0] // TOP_K

    def row_copy(step, t, k):
        d = dest_ref[k * n_tok + step * tm + t]
        slot = step % 2
        return pltpu.make_async_copy(
            yb_hbm.at[_row_slice(d)], buf_ref.at[slot, _row_slice(k * tm + t)], sem.at[slot])

    def issue(step):
        def body(t, carry):
            for k in range(TOP_K):
                row_copy(step, t, k).start(priority=k % DMA_QUEUES)
            return carry
        lax.fori_loop(0, tm, body, 0, unroll=8)

    def drain(step):
        def body(t, carry):
            for k in range(TOP_K):
                row_copy(step, t, k).wait()
            return carry
        lax.fori_loop(0, tm, body, 0, unroll=8)

    @pl.when(i == 0)
    def _():
        issue(i)

    @pl.when(i + 1 < pl.num_programs(0))
    def _():
        issue(i + 1)

    drain(i)

    r = route_ref[...]
    acc = h_ref[...]
    cur = buf_ref.at[i % 2]
    for k in range(TOP_K):
        acc = acc + r[:, 2 * TOP_K + k:2 * TOP_K + k + 1] * _load_row_tiled(cur, tm, k * tm)
    out = acc * lax.rsqrt(jnp.mean(acc * acc, axis=-1, keepdims=True) + EPS) * gain_ref[...]
    out_ref[...] = out


def _combine(dest_flat, route, h, gain, yb_rt):
    T = h.shape[0]
    tm = TM_COMBINE
    return pl.pallas_call(
        _combine_kernel,
        grid_spec=pltpu.PrefetchScalarGridSpec(
            num_scalar_prefetch=1,
            grid=(T // tm,),
            in_specs=[
                pl.BlockSpec((tm, LANES), lambda i, d: (i, 0)),
                pl.BlockSpec((tm, D_MODEL), lambda i, d: (i, 0)),
                pl.BlockSpec((1, D_MODEL), lambda i, d: (0, 0)),
                pl.BlockSpec(memory_space=pl.ANY),
            ],
            out_specs=pl.BlockSpec((tm, D_MODEL), lambda i, d: (i, 0)),
            scratch_shapes=[
                pltpu.VMEM((2, TOP_K * tm * ROW_TILES, LANES), F32),
                pltpu.SemaphoreType.DMA((2,)),
            ],
        ),
        out_shape=jax.ShapeDtypeStruct((T, D_MODEL), F32),
        compiler_params=_cparams(),
        name="combine",
    )(dest_flat, route, h, gain, yb_rt)


def _layer(x2, B, S, norm1_gain, w_in, conv_qk, b_igate, b_fgate, mlstm_norm_gain, gmlp_ln_gain,
           w_spatial, b_spatial, gmlp_out_gain, w_out, norm2_gain, w_router, b_router,
           w_up, b_up, w_down, b_down):
    T = B * S
    n_gate = 2 * M_HEADS
    g0 = 4 * D_MLSTM
    w_in_bf = w_in.astype(BF16)
    w_main = jnp.concatenate([w_in_bf[:, :g0], w_in_bf[:, g0 + n_gate:]], axis=1)
    w_gate = jnp.pad(w_in_bf[:, g0:g0 + n_gate], ((0, 0), (0, LANES - n_gate)))
    gbias = jnp.pad(jnp.concatenate([b_igate, b_fgate]), (0, LANES - n_gate)).reshape(1, LANES)

    main, gates = _inproj(x2, norm1_gain.reshape(1, D_MODEL), w_main, w_gate)
    h_m = _mlstm(main.reshape(B, S, D_MAIN), gates.reshape(B, S, LANES), conv_qk, gbias,
                 mlstm_norm_gain.reshape(1, D_MLSTM)).reshape(T, D_MLSTM)
    h_g = _gmlp(main, gmlp_ln_gain.reshape(1, D_GMLP), w_spatial, b_spatial.T,
                gmlp_out_gain.reshape(1, D_GMLP))

    triu = jnp.triu(jnp.ones((TM_ROUTE, TM_ROUTE), BF16))
    h, hn_rt, route, route_t, cnt = _outproj(h_m, h_g, x2, w_out.astype(BF16),
                                    norm2_gain.reshape(1, D_MODEL), w_router.T.astype(BF16),
                                    b_router.reshape(N_EXPERTS, 1), triu)

    counts = cnt[:, 0].astype(jnp.int32)
    padded = (counts + BM_FFN - 1) // BM_FFN * BM_FFN
    pend = jnp.cumsum(padded)
    pstart = pend - padded
    nb = (T * TOP_K) // BM_FFN + N_EXPERTS
    dest_t = _dest(route_t, pstart.astype(F32).reshape(N_EXPERTS, 1))
    dest_flat = dest_t[:TOP_K].reshape(TOP_K * T)

    xs_rt = _dispatch(dest_flat, pend.astype(jnp.int32), hn_rt, nb * BM_FFN)
    yb_rt = _ffn((pstart // BM_FFN).astype(jnp.int32), (padded // BM_FFN).astype(jnp.int32), xs_rt,
                 w_up, b_up.reshape(N_EXPERTS, 1, 2 * D_FF), w_down,
                 b_down.reshape(N_EXPERTS, 1, D_MODEL))
    return dest_flat, route, h, yb_rt


def kernel(x, norm1_gain, w_in, conv_qk, b_igate, b_fgate, mlstm_norm_gain, gmlp_ln_gain,
           w_spatial, b_spatial, gmlp_out_gain, w_out, norm2_gain, w_router, b_router,
           w_up, b_up, w_down, b_down, final_gain):
    B, S, D = x.shape
    depth = norm1_gain.shape[0]
    assert depth == 1 and D == D_MODEL and S % CHUNK == 0
    x2 = x.reshape(B * S, D)
    l = 0
    dest_flat, route, h, yb_rt = _layer(
        x2, B, S, norm1_gain[l], w_in[l], conv_qk[l], b_igate[l], b_fgate[l], mlstm_norm_gain[l],
        gmlp_ln_gain[l], w_spatial[l], b_spatial[l], gmlp_out_gain[l], w_out[l], norm2_gain[l],
        w_router[l], b_router[l], w_up[l], b_up[l], w_down[l], b_down[l])
    out = _combine(dest_flat, route, h, final_gain.reshape(1, D_MODEL), yb_rt)
    return out.reshape(B, S, D)
```

```python
import functools

import jax
import jax.numpy as jnp
from jax import lax
from jax.experimental import pallas as pl
from jax.experimental.pallas import tpu as pltpu

F32 = jnp.float32
BF16 = jnp.bfloat16

D_MODEL = 1024
M_HEADS = 4
M_HEAD_DIM = 128
D_MLSTM = M_HEADS * M_HEAD_DIM
D_GMLP = D_MODEL - D_MLSTM
G_GROUPS = 4
G_DIM = D_GMLP // G_GROUPS
CHUNK = 128
CONV_K = 4
N_EXPERTS = 32
TOP_K = 4
D_FF = D_MODEL
SWIGLU_LIMIT = 7.0
SWIGLU_ALPHA = 1.702
EPS = 1e-6

LANES = 128
SUBLANES = 8
ROW_TILES = D_MODEL // LANES
assert ROW_TILES == SUBLANES
D_MAIN = 4 * D_MLSTM + 2 * D_GMLP
TM_PROJ = 512
TG_GMLP = 512
TM_ROUTE = 512
TD_DISPATCH = 512
TM_COMBINE = 512
BM_FFN = 256
DMA_QUEUES = 2
VMEM_LIMIT = 56 * 1024 * 1024


def _cparams(n_axes=1):
    return pltpu.CompilerParams(
        dimension_semantics=("arbitrary",) * n_axes, vmem_limit_bytes=VMEM_LIMIT)


def _sigmoid(x):
    return 1.0 / (1.0 + jnp.exp(-x))


def _gelu_tanh(x):
    c = 0.7978845608028654
    return x * (0.5 * (1.0 + jnp.tanh(c * (x + 0.044715 * (x * x * x)))))


def _log_sigmoid(x):
    return jnp.minimum(x, 0.0) - jnp.log1p(jnp.exp(-jnp.abs(x)))


def _store_row_tiled(ref, x, row0=0):
    n_rows = x.shape[0]
    for g in range(n_rows // SUBLANES):
        for j in range(ROW_TILES):
            ref[pl.ds((row0 + g * SUBLANES) * ROW_TILES + j, SUBLANES, stride=ROW_TILES), :] = (
                x[g * SUBLANES:(g + 1) * SUBLANES, j * LANES:(j + 1) * LANES])


def _load_row_tiled(ref, n_rows, row0=0):
    groups = []
    for g in range(n_rows // SUBLANES):
        tiles = [ref[pl.ds((row0 + g * SUBLANES) * ROW_TILES + j, SUBLANES, stride=ROW_TILES), :]
                 for j in range(ROW_TILES)]
        groups.append(jnp.concatenate(tiles, axis=1))
    return jnp.concatenate(groups, axis=0)


def _inproj_kernel(x_ref, gain_ref, wm_ref, wg_ref, main_ref, gate_ref):
    x = x_ref[...]
    xn = x * lax.rsqrt(jnp.mean(x * x, axis=-1, keepdims=True) + EPS) * gain_ref[...]
    xb = xn.astype(BF16)
    main_ref[...] = jnp.dot(xb, wm_ref[...], preferred_element_type=F32).astype(BF16)
    gate_ref[...] = jnp.dot(xb, wg_ref[...], preferred_element_type=F32)


def _inproj(x2, gain, w_main, w_gate):
    T = x2.shape[0]
    return pl.pallas_call(
        _inproj_kernel,
        grid=(T // TM_PROJ,),
        in_specs=[
            pl.BlockSpec((TM_PROJ, D_MODEL), lambda i: (i, 0)),
            pl.BlockSpec((1, D_MODEL), lambda i: (0, 0)),
            pl.BlockSpec((D_MODEL, D_MAIN), lambda i: (0, 0)),
            pl.BlockSpec((D_MODEL, LANES), lambda i: (0, 0)),
        ],
        out_specs=[
            pl.BlockSpec((TM_PROJ, D_MAIN), lambda i: (i, 0)),
            pl.BlockSpec((TM_PROJ, LANES), lambda i: (i, 0)),
        ],
        out_shape=[
            jax.ShapeDtypeStruct((T, D_MAIN), BF16),
            jax.ShapeDtypeStruct((T, LANES), F32),
        ],
        compiler_params=_cparams(),
        name="inproj",
    )(x2, gain, w_main, w_gate)


def _mlstm_kernel(qkvo_ref, gates_ref, conv_ref, shift_ref, gbias_ref, ngain_ref, out_ref,
                  xp_ref, ct_ref, n_ref, m_ref, *, batch):
    c = pl.program_id(0)
    L, DH = CHUNK, M_HEAD_DIM

    @pl.when(c == 0)
    def _():
        xp_ref[...] = jnp.zeros_like(xp_ref)
        ct_ref[...] = jnp.zeros_like(ct_ref)
        n_ref[...] = jnp.zeros_like(n_ref)
        m_ref[...] = jnp.zeros_like(m_ref)

    row = lax.broadcasted_iota(jnp.int32, (L, L), 0)
    col = lax.broadcasted_iota(jnp.int32, (L, L), 1)
    tril = row >= col
    tril_f = tril.astype(F32)
    chains = [(b, h) for b in range(batch) for h in range(M_HEADS)]


    qk, gc, ball, gc_t, ball_t = [], [], [], [], []
    for b in range(batch):
        x_cur = qkvo_ref[b, :, 0:2 * D_MLSTM]
        shifted = jnp.dot(shift_ref[...], jnp.concatenate([xp_ref[b], x_cur], axis=0),
                          preferred_element_type=F32)
        acc = x_cur.astype(F32) * conv_ref[CONV_K - 1:CONV_K, :]
        for j in range(CONV_K - 1):
            acc = acc + shifted[j * L:(j + 1) * L, :] * conv_ref[j:j + 1, :]
        xp_ref[b] = x_cur
        qk.append(acc * _sigmoid(acc))
        g = gates_ref[b] + gbias_ref[...]
        gc.append(g)
        ball.append(jnp.dot(tril_f, _log_sigmoid(g), precision=lax.Precision.HIGHEST,
                            preferred_element_type=F32))
    for b in range(batch):
        gc_t.append(gc[b].T)
        ball_t.append(ball[b].T)

    st = []
    for b, h in chains:
        s = b * M_HEADS + h
        q = qk[b][:, h * DH:(h + 1) * DH]
        k = qk[b][:, D_MLSTM + h * DH:D_MLSTM + (h + 1) * DH] * (DH ** -0.5)
        d = dict(
            s=s, b=b, h=h, q=q, k=k, qb=q.astype(BF16), kb=k.astype(BF16),
            v=qkvo_ref[b, :, 2 * D_MLSTM + h * DH:2 * D_MLSTM + (h + 1) * DH],
            b_col=ball[b][:, M_HEADS + h:M_HEADS + h + 1],
            b_row=ball_t[b][M_HEADS + h:M_HEADS + h + 1, :],
            li_col=gc[b][:, h:h + 1],
            li_row=gc_t[b][h:h + 1, :],
            m_prev=m_ref[s][0:1, 0:1],
            ct_prev=ct_ref[s],
            n_prev=n_ref[s],
        )
        d["b_last"] = d["b_col"][L - 1:L, :]
        st.append(d)
    for d in st:
        d["qkt"] = lax.dot_general(d["qb"], d["kb"], (((1,), (1,)), ((), ())),
                                   preferred_element_type=F32)
        d["q_ct"] = jnp.dot(d["qb"], d["ct_prev"].astype(BF16), preferred_element_type=F32)
    for d in st:
        d["dmat"] = jnp.where(tril, d["b_col"] - d["b_row"] + d["li_row"], -jnp.inf)
        d["a_inter"] = d["b_col"] + d["m_prev"]
    for d in st:
        d["m_t"] = jnp.maximum(d["a_inter"], jnp.max(d["dmat"], axis=-1, keepdims=True))

    for d in st:
        d["w"] = jnp.exp(d["dmat"] - d["m_t"]) * d["qkt"]
        d["inter"] = jnp.exp(d["a_inter"] - d["m_t"])
    for d in st:
        d["num"] = (jnp.dot(d["w"].astype(BF16), d["v"], preferred_element_type=F32)
                    + d["inter"] * d["q_ct"])
        d["den"] = (jnp.sum(d["w"], axis=-1, keepdims=True)
                    + d["inter"] * jnp.sum(d["q"] * d["n_prev"], axis=-1, keepdims=True))
    for d in st:
        d["hh"] = d["num"] * (1.0 / jnp.maximum(jnp.abs(d["den"]), jnp.exp(-d["m_t"])))

    for d in st:
        d["mu"] = jnp.mean(d["hh"], axis=-1, keepdims=True)
    for d in st:
        d["xc"] = d["hh"] - d["mu"]
        d["var"] = jnp.mean(d["xc"] * d["xc"], axis=-1, keepdims=True)
    for d in st:
        b, h = d["b"], d["h"]
        y = d["xc"] * lax.rsqrt(d["var"] + EPS) * ngain_ref[:, h * DH:(h + 1) * DH]
        o = qkvo_ref[b, :, 3 * D_MLSTM + h * DH:3 * D_MLSTM + (h + 1) * DH].astype(F32)
        out_ref[b, :, h * DH:(h + 1) * DH] = (_sigmoid(o) * y).astype(out_ref.dtype)

    for d in st:
        d["g_col"] = d["b_last"] - d["b_col"] + d["li_col"]
        d["m_loc"] = jnp.max(d["g_col"], axis=0, keepdims=True)
    for d in st:
        d["w_col"] = jnp.exp(d["g_col"] - d["m_loc"])
        d["k_t"] = d["k"].T.astype(BF16)
    for d in st:
        wv = (d["w_col"] * d["v"].astype(F32)).astype(BF16)
        d["ct_loc"] = jnp.dot(d["k_t"], wv, preferred_element_type=F32)
        d["n_loc"] = jnp.sum(d["w_col"] * d["k"], axis=0, keepdims=True)
    for d in st:
        s = d["s"]
        m_new = jnp.maximum(d["b_last"] + d["m_prev"], d["m_loc"])
        a = jnp.exp(d["b_last"] + d["m_prev"] - m_new)
        cc = jnp.exp(d["m_loc"] - m_new)
        ct_ref[s] = a * d["ct_prev"] + cc * d["ct_loc"]
        n_ref[s] = a * d["n_prev"] + cc * d["n_loc"]
        m_ref[s] = jnp.broadcast_to(m_new, (SUBLANES, LANES))


def _mlstm(main3, gates3, conv_qk, gbias, ngain):
    B, S, _ = main3.shape
    nc = S // CHUNK
    t = jnp.arange(CHUNK)
    shift = jnp.concatenate(
        [(jnp.arange(2 * CHUNK)[None, :] == (CHUNK + t - (CONV_K - 1) + j)[:, None]).astype(BF16)
         for j in range(CONV_K - 1)], axis=0)
    return pl.pallas_call(
        functools.partial(_mlstm_kernel, batch=B),
        grid=(nc,),
        in_specs=[
            pl.BlockSpec((B, CHUNK, 4 * D_MLSTM), lambda c: (0, c, 0)),
            pl.BlockSpec((B, CHUNK, LANES), lambda c: (0, c, 0)),
            pl.BlockSpec((CONV_K, 2 * D_MLSTM), lambda c: (0, 0)),
            pl.BlockSpec(((CONV_K - 1) * CHUNK, 2 * CHUNK), lambda c: (0, 0)),
            pl.BlockSpec((1, LANES), lambda c: (0, 0)),
            pl.BlockSpec((1, D_MLSTM), lambda c: (0, 0)),
        ],
        out_specs=pl.BlockSpec((B, CHUNK, D_MLSTM), lambda c: (0, c, 0)),
        out_shape=jax.ShapeDtypeStruct((B, S, D_MLSTM), BF16),
        scratch_shapes=[
            pltpu.VMEM((B, CHUNK, 2 * D_MLSTM), BF16),
            pltpu.VMEM((B * M_HEADS, M_HEAD_DIM, M_HEAD_DIM), F32),
            pltpu.VMEM((B * M_HEADS, 1, M_HEAD_DIM), F32),
            pltpu.VMEM((B * M_HEADS, SUBLANES, LANES), F32),
        ],
        compiler_params=_cparams(),
        name="mlstm",
    )(main3, gates3, conv_qk, shift, gbias, ngain)


def _gmlp_kernel(uv_ref, lng_ref, ws_ref, bs_ref, og_ref, out_ref, prod_ref):
    L = CHUNK
    u = _gelu_tanh(uv_ref[:, 0:D_GMLP].astype(F32))
    vg = _gelu_tanh(uv_ref[:, D_GMLP:2 * D_GMLP].astype(F32))
    mu = jnp.mean(vg, axis=-1, keepdims=True)
    xc = vg - mu
    vg = xc * lax.rsqrt(jnp.mean(xc * xc, axis=-1, keepdims=True) + EPS) * lng_ref[...]
    vb = vg.astype(BF16)
    row = lax.broadcasted_iota(jnp.int32, (L, L), 0)
    col = lax.broadcasted_iota(jnp.int32, (L, L), 1)
    tril = row >= col
    for g in range(G_GROUPS):
        wg = jnp.where(tril, ws_ref[g], 0.0).astype(BF16)
        b_col = bs_ref[:, g:g + 1]
        for j in range(TG_GMLP // L):
            mixed = jnp.dot(wg, vb[j * L:(j + 1) * L, g * G_DIM:(g + 1) * G_DIM],
                            preferred_element_type=F32) + b_col
            prod_ref[j * L:(j + 1) * L, g * G_DIM:(g + 1) * G_DIM] = (
                u[j * L:(j + 1) * L, g * G_DIM:(g + 1) * G_DIM] * mixed)
    p = prod_ref[...]
    y = p * lax.rsqrt(jnp.mean(p * p, axis=-1, keepdims=True) + EPS) * og_ref[...]
    out_ref[...] = y.astype(out_ref.dtype)


def _gmlp(main, ln_gain, w_spatial, b_spatial_t, out_gain):
    T = main.shape[0]
    uv_block = 4 * D_MLSTM // (2 * D_GMLP)
    return pl.pallas_call(
        _gmlp_kernel,
        grid=(T // TG_GMLP,),
        in_specs=[
            pl.BlockSpec((TG_GMLP, 2 * D_GMLP), lambda i: (i, uv_block)),
            pl.BlockSpec((1, D_GMLP), lambda i: (0, 0)),
            pl.BlockSpec((G_GROUPS, CHUNK, CHUNK), lambda i: (0, 0, 0)),
            pl.BlockSpec((CHUNK, G_GROUPS), lambda i: (0, 0)),
            pl.BlockSpec((1, D_GMLP), lambda i: (0, 0)),
        ],
        out_specs=pl.BlockSpec((TG_GMLP, D_GMLP), lambda i: (i, 0)),
        out_shape=jax.ShapeDtypeStruct((T, D_GMLP), BF16),
        scratch_shapes=[pltpu.VMEM((TG_GMLP, D_GMLP), F32)],
        compiler_params=_cparams(),
        name="gmlp",
    )(main, ln_gain, w_spatial, b_spatial_t, out_gain)


ROUTE_ROWS = 16


def _outproj_kernel(hm_ref, hg_ref, x_ref, wo_ref, g2_ref, wrt_ref, brt_ref, triu_ref,
                    h_ref, hn_ref, route_ref, route_t_ref, cnt_ref, carry_ref):
    i = pl.program_id(0)
    tm = TM_ROUTE

    @pl.when(i == 0)
    def _():
        carry_ref[...] = jnp.zeros_like(carry_ref)

    y = (jnp.dot(hm_ref[...], wo_ref[0:D_MLSTM, :], preferred_element_type=F32)
         + jnp.dot(hg_ref[...], wo_ref[D_MLSTM:D_MODEL, :], preferred_element_type=F32))
    h = x_ref[...] + y
    h_ref[...] = h
    hn = h * lax.rsqrt(jnp.mean(h * h, axis=-1, keepdims=True) + EPS) * g2_ref[...]
    _store_row_tiled(hn_ref, hn)

    lt = lax.dot_general(wrt_ref[...], hn.astype(BF16), (((1,), (1,)), ((), ())),
                         preferred_element_type=F32) + brt_ref[...]
    e_id = lax.broadcasted_iota(jnp.int32, (N_EXPERTS, tm), 0).astype(F32)
    l = lt
    top_v, top_i = [], []
    for _ in range(TOP_K):
        mx = jnp.max(l, axis=0, keepdims=True)
        ix = jnp.min(jnp.where(l == mx, e_id, float(N_EXPERTS)), axis=0, keepdims=True)
        top_v.append(mx)
        top_i.append(ix)
        l = jnp.where(e_id == ix, -jnp.inf, l)
    ex = [jnp.exp(v - top_v[0]) for v in top_v]
    inv = 1.0 / (ex[0] + ex[1] + ex[2] + ex[3])

    onehot = [e_id == ix for ix in top_i]
    mask = (onehot[0].astype(F32) + onehot[1].astype(F32)
            + onehot[2].astype(F32) + onehot[3].astype(F32))
    incl = jnp.dot(mask.astype(BF16), triu_ref[...], preferred_element_type=F32)
    base = carry_ref[:, 0:1] + incl - mask
    ranks = [jnp.sum(jnp.where(onehot[k], base, 0.0), axis=0, keepdims=True)
             for k in range(TOP_K)]
    rows = top_i + ranks + [ex[k] * inv for k in range(TOP_K)]
    route_t = jnp.concatenate(rows + [jnp.zeros((LANES - len(rows), tm), F32)], axis=0)
    route_ref[...] = route_t.T
    route_t_ref[...] = route_t[0:ROUTE_ROWS, :]
    carry_ref[...] = carry_ref[...] + incl[:, tm - 1:tm]
    cnt_ref[...] = carry_ref[...]


def _outproj(h_m, h_g, x2, w_out, gain2, w_router_t, b_router_col, triu):
    T = x2.shape[0]
    tm = TM_ROUTE
    return pl.pallas_call(
        _outproj_kernel,
        grid=(T // tm,),
        in_specs=[
            pl.BlockSpec((tm, D_MLSTM), lambda i: (i, 0)),
            pl.BlockSpec((tm, D_GMLP), lambda i: (i, 0)),
            pl.BlockSpec((tm, D_MODEL), lambda i: (i, 0)),
            pl.BlockSpec((D_MODEL, D_MODEL), lambda i: (0, 0)),
            pl.BlockSpec((1, D_MODEL), lambda i: (0, 0)),
            pl.BlockSpec((N_EXPERTS, D_MODEL), lambda i: (0, 0)),
            pl.BlockSpec((N_EXPERTS, 1), lambda i: (0, 0)),
            pl.BlockSpec((tm, tm), lambda i: (0, 0)),
        ],
        out_specs=[
            pl.BlockSpec((tm, D_MODEL), lambda i: (i, 0)),
            pl.BlockSpec((tm * ROW_TILES, LANES), lambda i: (i, 0)),
            pl.BlockSpec((tm, LANES), lambda i: (i, 0)),
            pl.BlockSpec((ROUTE_ROWS, tm), lambda i: (0, i)),
            pl.BlockSpec((N_EXPERTS, LANES), lambda i: (0, 0)),
        ],
        out_shape=[
            jax.ShapeDtypeStruct((T, D_MODEL), F32),
            jax.ShapeDtypeStruct((T * ROW_TILES, LANES), F32),
            jax.ShapeDtypeStruct((T, LANES), F32),
            jax.ShapeDtypeStruct((ROUTE_ROWS, T), F32),
            jax.ShapeDtypeStruct((N_EXPERTS, LANES), F32),
        ],
        scratch_shapes=[pltpu.VMEM((N_EXPERTS, LANES), F32)],
        compiler_params=_cparams(),
        name="outproj_router",
    )(h_m, h_g, x2, w_out, gain2, w_router_t, b_router_col, triu)


TM_DEST = 2048


def _dest_kernel(route_t_ref, pstart_ref, dest_ref):
    r = route_t_ref[...]
    tm = r.shape[1]
    e_id = lax.broadcasted_iota(jnp.int32, (N_EXPERTS, tm), 0).astype(F32)
    ps = pstart_ref[...]
    rows = []
    for k in range(TOP_K):
        start = jnp.sum(jnp.where(e_id == r[k:k + 1, :], ps, 0.0), axis=0, keepdims=True)
        rows.append(r[TOP_K + k:TOP_K + k + 1, :] + start)
    rows.append(jnp.zeros((SUBLANES - TOP_K, tm), F32))
    dest_ref[...] = jnp.concatenate(rows, axis=0).astype(jnp.int32)


def _dest(route_t, pstart_col):
    T = route_t.shape[1]
    tm = TM_DEST
    return pl.pallas_call(
        _dest_kernel,
        grid=(T // tm,),
        in_specs=[
            pl.BlockSpec((ROUTE_ROWS, tm), lambda i: (0, i)),
            pl.BlockSpec((N_EXPERTS, 1), lambda i: (0, 0)),
        ],
        out_specs=pl.BlockSpec((SUBLANES, tm), lambda i: (0, i)),
        out_shape=jax.ShapeDtypeStruct((SUBLANES, T), jnp.int32),
        compiler_params=_cparams(),
        name="dest",
    )(route_t, pstart_col)


def _row_slice(n):
    return pl.ds(pl.multiple_of(n * ROW_TILES, ROW_TILES), ROW_TILES)


def _dispatch_kernel(dest_ref, pend_ref, hn_ref, xs_hbm, zero_ref, src_ref, sem, zsem):
    i = pl.program_id(0)
    base = i * TD_DISPATCH
    blk_rows = BM_FFN * ROW_TILES

    @pl.when(i == 0)
    def _():
        zero_ref[...] = jnp.zeros_like(zero_ref)

        def zero_copy(e):
            start = pl.multiple_of((pend_ref[e] - BM_FFN) * ROW_TILES, blk_rows)
            return pltpu.make_async_copy(zero_ref, xs_hbm.at[pl.ds(start, blk_rows)], zsem)

        def nonempty(e):
            return pend_ref[e] > (pend_ref[e - 1] if e > 0 else 0)

        nb = xs_hbm.shape[0] // blk_rows

        def tail_copy(j):
            return pltpu.make_async_copy(
                zero_ref, xs_hbm.at[pl.ds((nb - 1 - j) * blk_rows, blk_rows)], zsem)

        def unused(j):
            return (nb - 1 - j) * BM_FFN >= pend_ref[N_EXPERTS - 1]

        for e in range(N_EXPERTS):
            @pl.when(nonempty(e))
            def _():
                zero_copy(e).start()

            @pl.when(unused(e))
            def _():
                tail_copy(e).start()
        for e in range(N_EXPERTS):
            @pl.when(nonempty(e))
            def _():
                zero_copy(e).wait()

            @pl.when(unused(e))
            def _():
                tail_copy(e).wait()

    slot = i % 2
    src_ref[slot] = hn_ref[...]

    n_tok = dest_ref.shape[0] // TOP_K

    def row_copy(s, t, k):
        d = dest_ref[k * n_tok + base + t]
        return pltpu.make_async_copy(src_ref.at[s, _row_slice(t)], xs_hbm.at[_row_slice(d)],
                                     sem.at[s])

    def issue(t, carry):
        for k in range(TOP_K):
            row_copy(slot, t, k).start(priority=k % DMA_QUEUES)
        return carry

    def drain(s):
        def body(t, carry):
            for k in range(TOP_K):
                row_copy(s, t, k).wait()
            return carry
        lax.fori_loop(0, TD_DISPATCH, body, 0, unroll=8)

    lax.fori_loop(0, TD_DISPATCH, issue, 0, unroll=8)

    @pl.when(i > 0)
    def _():
        drain(1 - slot)

    @pl.when(i == pl.num_programs(0) - 1)
    def _():
        drain(slot)


def _dispatch(dest_flat, pend, hn_rt, nbuf):
    T = hn_rt.shape[0] // ROW_TILES
    return pl.pallas_call(
        _dispatch_kernel,
        grid_spec=pltpu.PrefetchScalarGridSpec(
            num_scalar_prefetch=2,
            grid=(T // TD_DISPATCH,),
            in_specs=[pl.BlockSpec((TD_DISPATCH * ROW_TILES, LANES), lambda i, d, p: (i, 0))],
            out_specs=pl.BlockSpec(memory_space=pl.ANY),
            scratch_shapes=[
                pltpu.VMEM((BM_FFN * ROW_TILES, LANES), F32),
                pltpu.VMEM((2, TD_DISPATCH * ROW_TILES, LANES), F32),
                pltpu.SemaphoreType.DMA((2,)),
                pltpu.SemaphoreType.DMA,
            ],
        ),
        out_shape=jax.ShapeDtypeStruct((nbuf * ROW_TILES, LANES), F32),
        compiler_params=_cparams(),
        name="dispatch",
    )(dest_flat, pend, hn_rt)


def _ffn_kernel(bstart_ref, bcnt_ref, xs_hbm, wup_ref, bup_ref, wdn_ref, bdn_ref, y_hbm,
                xbuf, ybuf, wup_bf, wdn_bf, in_sem, out_sem):
    e = pl.program_id(0)
    blk_rows = BM_FFN * ROW_TILES
    b0 = bstart_ref[e]
    cnt = bcnt_ref[e]

    def block_rows(j):
        return pl.ds(pl.multiple_of(j * blk_rows, blk_rows), blk_rows)

    def in_copy(j, slot):
        return pltpu.make_async_copy(xs_hbm.at[block_rows(j)], xbuf.at[slot], in_sem.at[slot])

    def out_copy(j, slot):
        return pltpu.make_async_copy(ybuf.at[slot], y_hbm.at[block_rows(j)], out_sem.at[slot])

    @pl.when(cnt > 0)
    def _():
        in_copy(b0, 0).start(priority=1)
        wup_bf[...] = wup_ref[0].astype(BF16)
        wdn_bf[...] = wdn_ref[0].astype(BF16)

        def body(j, carry):
            slot = j % 2
            in_copy(b0 + j, slot).wait()

            @pl.when(j + 1 < cnt)
            def _():
                in_copy(b0 + j + 1, 1 - slot).start(priority=1)

            @pl.when(j >= 2)
            def _():
                out_copy(b0 + j - 2, slot).wait()

            xb = _load_row_tiled(xbuf.at[slot], BM_FFN).astype(BF16)
            hb = jnp.dot(xb, wup_bf[...], preferred_element_type=F32) + bup_ref[0]
            glu = jnp.minimum(hb[:, 0:D_FF], SWIGLU_LIMIT)
            lin = jnp.clip(hb[:, D_FF:2 * D_FF], -SWIGLU_LIMIT, SWIGLU_LIMIT)
            act = glu * _sigmoid(SWIGLU_ALPHA * glu) * (lin + 1.0)
            y = jnp.dot(act.astype(BF16), wdn_bf[...], preferred_element_type=F32) + bdn_ref[0]
            _store_row_tiled(ybuf.at[slot], y)
            out_copy(b0 + j, slot).start(priority=1)
            return carry

        lax.fori_loop(0, cnt, body, 0)

        @pl.when(cnt >= 2)
        def _():
            out_copy(b0 + cnt - 2, cnt % 2).wait()

        out_copy(b0 + cnt - 1, (cnt - 1) % 2).wait()

    @pl.when(e == N_EXPERTS - 1)
    def _():
        nb = y_hbm.shape[0] // blk_rows
        n_valid = b0 + cnt
        ybuf[0] = jnp.zeros((blk_rows, LANES), F32)
        for j in range(N_EXPERTS):
            @pl.when(nb - 1 - j >= n_valid)
            def _():
                out_copy(nb - 1 - j, 0).start()
        for j in range(N_EXPERTS):
            @pl.when(nb - 1 - j >= n_valid)
            def _():
                out_copy(nb - 1 - j, 0).wait()


def _ffn(bstart, bcnt, xs_rt, w_up, b_up3, w_down, b_down3):
    blk_rows = BM_FFN * ROW_TILES
    wspec = lambda shape: pl.BlockSpec(shape, lambda e, bs, bc: (e, 0, 0))
    return pl.pallas_call(
        _ffn_kernel,
        grid_spec=pltpu.PrefetchScalarGridSpec(
            num_scalar_prefetch=2,
            grid=(N_EXPERTS,),
            in_specs=[
                pl.BlockSpec(memory_space=pl.ANY),
                wspec((1, D_MODEL, 2 * D_FF)),
                wspec((1, 1, 2 * D_FF)),
                wspec((1, D_FF, D_MODEL)),
                wspec((1, 1, D_MODEL)),
            ],
            out_specs=pl.BlockSpec(memory_space=pl.ANY),
            scratch_shapes=[
                pltpu.VMEM((2, blk_rows, LANES), F32),
                pltpu.VMEM((2, blk_rows, LANES), F32),
                pltpu.VMEM((D_MODEL, 2 * D_FF), BF16),
                pltpu.VMEM((D_FF, D_MODEL), BF16),
                pltpu.SemaphoreType.DMA((2,)),
                pltpu.SemaphoreType.DMA((2,)),
            ],
        ),
        out_shape=jax.ShapeDtypeStruct(xs_rt.shape, F32),
        compiler_params=_cparams(),
        name="expert_ffn",
    )(bstart, bcnt, xs_rt, w_up, b_up3, w_down, b_down3)


def _combine_kernel(dest_ref, route_ref, h_ref, gain_ref, yb_hbm, out_ref, buf_ref, sem):
    tm = TM_COMBINE
    i = pl.program_id(0)

    n_tok = dest_ref.shape[0] // TOP_K

    def row_copy(step, t, k):
        d = dest_ref[k * n_tok + step * tm + t]
        slot = step % 2
        return pltpu.make_async_copy(
            yb_hbm.at[_row_slice(d)], buf_ref.at[slot, _row_slice(k * tm + t)], sem.at[slot])

    def issue(step):
        def body(t, carry):
            for k in range(TOP_K):
                row_copy(step, t, k).start(priority=k % DMA_QUEUES)
            return carry
        lax.fori_loop(0, tm, body, 0, unroll=8)

    def drain(step):
        def body(t, carry):
            for k in range(TOP_K):
                row_copy(step, t, k).wait()
            return carry
        lax.fori_loop(0, tm, body, 0, unroll=8)

    @pl.when(i == 0)
    def _():
        issue(i)

    @pl.when(i + 1 < pl.num_programs(0))
    def _():
        issue(i + 1)

    drain(i)

    r = route_ref[...]
    acc = h_ref[...]
    cur = buf_ref.at[i % 2]
    for k in range(TOP_K):
        acc = acc + r[:, 2 * TOP_K + k:2 * TOP_K + k + 1] * _load_row_tiled(cur, tm, k * tm)
    out = acc * lax.rsqrt(jnp.mean(acc * acc, axis=-1, keepdims=True) + EPS) * gain_ref[...]
    out_ref[...] = out


def _combine(dest_flat, route, h, gain, yb_rt):
    T = h.shape[0]
    tm = TM_COMBINE
    return pl.pallas_call(
        _combine_kernel,
        grid_spec=pltpu.PrefetchScalarGridSpec(
            num_scalar_prefetch=1,
            grid=(T // tm,),
            in_specs=[
                pl.BlockSpec((tm, LANES), lambda i, d: (i, 0)),
                pl.BlockSpec((tm, D_MODEL), lambda i, d: (i, 0)),
                pl.BlockSpec((1, D_MODEL), lambda i, d: (0, 0)),
                pl.BlockSpec(memory_space=pl.ANY),
            ],
            out_specs=pl.BlockSpec((tm, D_MODEL), lambda i, d: (i, 0)),
            scratch_shapes=[
                pltpu.VMEM((2, TOP_K * tm * ROW_TILES, LANES), F32),
                pltpu.SemaphoreType.DMA((2,)),
            ],
        ),
        out_shape=jax.ShapeDtypeStruct((T, D_MODEL), F32),
        compiler_params=_cparams(),
        name="combine",
    )(dest_flat, route, h, gain, yb_rt)


def _layer(x2, B, S, norm1_gain, w_in, conv_qk, b_igate, b_fgate, mlstm_norm_gain, gmlp_ln_gain,
           w_spatial, b_spatial, gmlp_out_gain, w_out, norm2_gain, w_router, b_router,
           w_up, b_up, w_down, b_down):
    T = B * S
    n_gate = 2 * M_HEADS
    g0 = 4 * D_MLSTM
    w_in_bf = w_in.astype(BF16)
    w_main = jnp.concatenate([w_in_bf[:, :g0], w_in_bf[:, g0 + n_gate:]], axis=1)
    w_gate = jnp.pad(w_in_bf[:, g0:g0 + n_gate], ((0, 0), (0, LANES - n_gate)))
    gbias = jnp.pad(jnp.concatenate([b_igate, b_fgate]), (0, LANES - n_gate)).reshape(1, LANES)

    main, gates = _inproj(x2, norm1_gain.reshape(1, D_MODEL), w_main, w_gate)
    h_m = _mlstm(main.reshape(B, S, D_MAIN), gates.reshape(B, S, LANES), conv_qk, gbias,
                 mlstm_norm_gain.reshape(1, D_MLSTM)).reshape(T, D_MLSTM)
    h_g = _gmlp(main, gmlp_ln_gain.reshape(1, D_GMLP), w_spatial, b_spatial.T,
                gmlp_out_gain.reshape(1, D_GMLP))

    triu = jnp.triu(jnp.ones((TM_ROUTE, TM_ROUTE), BF16))
    h, hn_rt, route, route_t, cnt = _outproj(h_m, h_g, x2, w_out.astype(BF16),
                                    norm2_gain.reshape(1, D_MODEL), w_router.T.astype(BF16),
                                    b_router.reshape(N_EXPERTS, 1), triu)

    counts = cnt[:, 0].astype(jnp.int32)
    padded = (counts + BM_FFN - 1) // BM_FFN * BM_FFN
    pend = jnp.cumsum(padded)
    pstart = pend - padded
    nb = (T * TOP_K) // BM_FFN + N_EXPERTS
    dest_t = _dest(route_t, pstart.astype(F32).reshape(N_EXPERTS, 1))
    dest_flat = dest_t[:TOP_K].reshape(TOP_K * T)

    xs_rt = _dispatch(dest_flat, pend.astype(jnp.int32), hn_rt, nb * BM_FFN)
    yb_rt = _ffn((pstart // BM_FFN).astype(jnp.int32), (padded // BM_FFN).astype(jnp.int32), xs_rt,
                 w_up, b_up.reshape(N_EXPERTS, 1, 2 * D_FF), w_down,
                 b_down.reshape(N_EXPERTS, 1, D_MODEL))
    return dest_flat, route, h, yb_rt


def kernel(x, norm1_gain, w_in, conv_qk, b_igate, b_fgate, mlstm_norm_gain, gmlp_ln_gain,
           w_spatial, b_spatial, gmlp_out_gain, w_out, norm2_gain, w_router, b_router,
           w_up, b_up, w_down, b_down, final_gain):
    B, S, D = x.shape
    depth = norm1_gain.shape[0]
    assert depth == 1 and D == D_MODEL and S % CHUNK == 0
    x2 = x.reshape(B * S, D)
    l = 0
    dest_flat, route, h, yb_rt = _layer(
        x2, B, S, norm1_gain[l], w_in[l], conv_qk[l], b_igate[l], b_fgate[l], mlstm_norm_gain[l],
        gmlp_ln_gain[l], w_spatial[l], b_spatial[l], gmlp_out_gain[l], w_out[l], norm2_gain[l],
        w_router[l], b_router[l], w_up[l], b_up[l], w_down[l], b_down[l])
    out = _combine(dest_flat, route, h, final_gain.reshape(1, D_MODEL), yb_rt)
    return out.reshape(B, S, D)
```
